```python
import numpy as np
import jax, jax.numpy as jnp
from jax import lax

D_MODEL = 2048
BATCH = 4
SEQ = 2048
DEPTH = 2

D_FF = 5632
EPS = 1e-6
CONV_CH = D_MODEL // 2
CONV_GROUPS = 8
CONV_WIDTH = 31
GMLP_CH = D_MODEL // 2
GMLP_GROUPS = 8
GMLP_GDIM = GMLP_CH // GMLP_GROUPS
GMLP_CHUNK = 128
HEAD_DIM = 128
N_HEADS = D_MODEL // HEAD_DIM
N_KV = 4
HPG = N_HEADS // N_KV
KV_W = N_KV * HEAD_DIM
ROT_DIM = HEAD_DIM // 4
ROPE_THETA = 500000.0
CMP_LEN = 32
CMP_STRIDE = 16
CMP_HIDDEN = 256
SEL_BLOCK = 64
SEL_TOPK = 16
WINDOW = 512
Q_BLOCK = 128
SEL_Q_CHUNK = 16
FORCE_BONUS = 1e3
NEG = -1e30
NSA_IN = D_MODEL + 6 * KV_W + 3 * N_HEADS
AB_IN = 2 * CONV_CH + 2 * GMLP_CH

kernel_name = 'hybrid_conv_gmlp_nsa_macaron'


def rmsnorm(x, g):
    xf = x.astype(jnp.float32)
    y = xf * lax.rsqrt(jnp.mean(xf * xf, axis=-1, keepdims=True) + EPS)
    return y.astype(x.dtype) * g


def layernorm(x, g, b):
    xf = x.astype(jnp.float32)
    mu = jnp.mean(xf, axis=-1, keepdims=True)
    var = jnp.mean(jnp.square(xf - mu), axis=-1, keepdims=True)
    return ((xf - mu) * lax.rsqrt(var + EPS)).astype(x.dtype) * g + b


def swiglu(h, w_gate, w_up, w_down):
    return (jax.nn.silu(h @ w_gate) * (h @ w_up)) @ w_down


def rope_tables(positions):
    inv = ROPE_THETA ** (-jnp.arange(0, ROT_DIM, 2, dtype=jnp.float32) / ROT_DIM)
    ang = positions.astype(jnp.float32)[..., None] * inv
    return jnp.cos(ang), jnp.sin(ang)


def apply_rope(x, cos, sin):
    half = ROT_DIM // 2
    cos = cos.astype(x.dtype)
    sin = sin.astype(x.dtype)
    x1, x2, rest = x[..., :half], x[..., half:ROT_DIM], x[..., ROT_DIM:]
    return jnp.concatenate([x1 * cos - x2 * sin, x2 * cos + x1 * sin, rest], axis=-1)


def conv_gmlp_mixer(h, w_in, conv_w, conv_b, conv_ln_g, conv_ln_b,
                    gmlp_ln_g, gmlp_ln_b, gmlp_ws, gmlp_bs, w_out):
    B_, S, _ = h.shape
    z = h @ w_in
    a_val, a_gate, b_u, b_v = jnp.split(z, [CONV_CH, 2 * CONV_CH, 2 * CONV_CH + GMLP_CH], axis=-1)
    a = a_val * jax.nn.sigmoid(a_gate)
    a = lax.conv_general_dilated(a, conv_w, (1,), [(CONV_WIDTH - 1, 0)],
                                 dimension_numbers=('NWC', 'WIO', 'NWC'),
                                 feature_group_count=CONV_CH) + conv_b
    a = jax.nn.silu(layernorm(a, conv_ln_g, conv_ln_b))
    nc = S // GMLP_CHUNK
    u = jax.nn.gelu(b_u).reshape(B_, nc, GMLP_CHUNK, GMLP_GROUPS, GMLP_GDIM)
    v = jax.nn.gelu(b_v).reshape(B_, nc, GMLP_CHUNK, GMLP_GROUPS, GMLP_GDIM)
    v = layernorm(v, gmlp_ln_g, gmlp_ln_b)
    tri = jnp.tril(jnp.ones((GMLP_CHUNK, GMLP_CHUNK), dtype=bool))
    ws = jnp.where(tri[None], gmlp_ws, 0.0).astype(v.dtype)
    sv = jnp.einsum('gts,bcsgd->bctgd', ws, v) + gmlp_bs.T[:, :, None]
    bo = (u * sv).reshape(B_, S, GMLP_CH)
    return jnp.concatenate([a, bo], axis=-1) @ w_out


def nsa_mixer(h, cos, sin, w_in, cmp_pe_k, cmp_w1_k, cmp_w2_k,
              cmp_pe_v, cmp_w1_v, cmp_w2_v, w_out):
    B_, S, _ = h.shape
    f32 = jnp.float32
    scale = HEAD_DIM ** -0.5
    z = h @ w_in
    offs = [D_MODEL + i * KV_W for i in range(7)]
    q, kc, vc, ks, vs, kw, vw, g = jnp.split(z, offs, axis=-1)
    q = q.reshape(B_, S, N_KV, HPG, HEAD_DIM)
    kc, vc, ks, vs, kw, vw = [t.reshape(B_, S, N_KV, HEAD_DIM) for t in (kc, vc, ks, vs, kw, vw)]
    gates = jax.nn.sigmoid(g.reshape(B_, S, N_KV, HPG, 3))
    q_r = apply_rope(q, cos[:, :, None, None], sin[:, :, None, None])
    ks_r = apply_rope(ks, cos[:, :, None], sin[:, :, None])
    kw_r = apply_rope(kw, cos[:, :, None], sin[:, :, None])
    t_np = np.arange(S)

    n_cmp = (S - CMP_LEN) // CMP_STRIDE + 1
    cidx = np.arange(n_cmp)[:, None] * CMP_STRIDE + np.arange(CMP_LEN)[None]

    def compress(k, pe, w1, w2):
        blk = k[:, cidx] + pe[:, None, :]
        blk = blk.transpose(0, 1, 3, 2, 4).reshape(B_, n_cmp, N_KV, CMP_LEN * HEAD_DIM)
        return jax.nn.gelu(blk @ w1) @ w2

    kcmp = compress(kc, cmp_pe_k, cmp_w1_k, cmp_w2_k)
    vcmp = compress(vc, cmp_pe_v, cmp_w1_v, cmp_w2_v)
    cend = cidx[:, -1]
    cmask = jnp.asarray(cend[None, :] <= t_np[:, None])
    s_c = jnp.einsum('btghd,bngd->bghtn', q, kcmp, preferred_element_type=f32) * scale
    p_c = jax.nn.softmax(jnp.where(cmask, s_c, NEG), axis=-1) * cmask
    o_c = jnp.einsum('bghtn,bngd->btghd', p_c.astype(vcmp.dtype), vcmp)

    n_blk = S // SEL_BLOCK
    n_sel = min(SEL_TOPK, n_blk)
    sb = np.arange(n_blk) * SEL_BLOCK
    overlap = (cidx[:, 0][:, None] < sb[None] + SEL_BLOCK) & (cend[:, None] >= sb[None])
    imp = jnp.einsum('bghtn,nj->btgj', p_c, jnp.asarray(overlap, f32))
    jb = np.arange(n_blk)[None]
    cur = (t_np // SEL_BLOCK)[:, None]
    forced = (jb == 0) | (jb == cur) | (jb == cur - 1)
    valid = jb * SEL_BLOCK <= t_np[:, None]
    imp = imp + FORCE_BONUS * jnp.asarray(forced, f32)[None, :, None, :]
    imp = jnp.where(jnp.asarray(valid)[None, :, None, :], imp, NEG)
    _, sel_idx = lax.top_k(imp, n_sel)

    ksb = ks_r.reshape(B_, n_blk, SEL_BLOCK, N_KV, HEAD_DIM).transpose(0, 3, 1, 2, 4)
    vsb = vs.reshape(B_, n_blk, SEL_BLOCK, N_KV, HEAD_DIM).transpose(0, 3, 1, 2, 4)
    bi = jnp.arange(B_)[:, None, None, None]
    gi = jnp.arange(N_KV)[None, None, :, None]

    def sel_chunk(args):
        qc, idx, tc = args
        kg = ksb[bi, gi, idx]
        vg = vsb[bi, gi, idx]
        kpos = idx[..., None] * SEL_BLOCK + jnp.arange(SEL_BLOCK)
        m = kpos <= tc[None, :, None, None, None]
        s = jnp.einsum('bqghd,bqgnld->bqghnl', qc, kg, preferred_element_type=f32) * scale
        s = jnp.where(m[:, :, :, None], s, NEG)
        p = jax.nn.softmax(s, axis=(-2, -1))
        return jnp.einsum('bqghnl,bqgnld->bqghd', p.astype(vg.dtype), vg)

    nq = S // SEL_Q_CHUNK
    qc_all = q_r.reshape(B_, nq, SEL_Q_CHUNK, N_KV, HPG, HEAD_DIM).transpose(1, 0, 2, 3, 4, 5)
    idx_all = sel_idx.reshape(B_, nq, SEL_Q_CHUNK, N_KV, n_sel).transpose(1, 0, 2, 3, 4)
    tc_all = jnp.arange(S, dtype=jnp.int32).reshape(nq, SEL_Q_CHUNK)
    o_s = lax.map(sel_chunk, (qc_all, idx_all, tc_all))
    o_s = o_s.transpose(1, 0, 2, 3, 4, 5).reshape(B_, S, N_KV, HPG, HEAD_DIM)

    nqb = S // Q_BLOCK
    span = WINDOW + Q_BLOCK
    widx = np.arange(nqb)[:, None] * Q_BLOCK + np.arange(span)[None]
    kpad = jnp.pad(kw_r, ((0, 0), (WINDOW, 0), (0, 0), (0, 0)))
    vpad = jnp.pad(vw, ((0, 0), (WINDOW, 0), (0, 0), (0, 0)))
    kwin = kpad[:, widx]
    vwin = vpad[:, widx]
    n_ = np.arange(nqb)[:, None, None]
    q_ = np.arange(Q_BLOCK)[None, :, None]
    k_ = np.arange(span)[None, None, :]
    kpos = n_ * Q_BLOCK + k_ - WINDOW
    qpos = n_ * Q_BLOCK + q_
    wmask = jnp.asarray((kpos <= qpos) & (kpos > qpos - WINDOW) & (kpos >= 0))
    qb = q_r.reshape(B_, nqb, Q_BLOCK, N_KV, HPG, HEAD_DIM)
    s_w = jnp.einsum('bnqghd,bnkgd->bnghqk', qb, kwin, preferred_element_type=f32) * scale
    s_w = jnp.where(wmask[None, :, None, None], s_w, NEG)
    p_w = jax.nn.softmax(s_w, axis=-1)
    o_w = jnp.einsum('bnghqk,bnkgd->bnqghd', p_w.astype(vwin.dtype), vwin)
    o_w = o_w.reshape(B_, S, N_KV, HPG, HEAD_DIM)

    o = gates[..., 0:1] * o_c + gates[..., 1:2] * o_s + gates[..., 2:3] * o_w
    return o.reshape(B_, S, D_MODEL) @ w_out


def setup_inputs(seed: int = 0) -> dict:
    key = jax.random.key(seed)
    ks = iter(jax.random.split(key, 64))
    f32 = jnp.float32

    def dense(shape, fan_in):
        return jax.random.normal(next(ks), shape, f32) * fan_in ** -0.5

    def gain(shape):
        return 1.0 + 0.02 * jax.random.normal(next(ks), shape, f32)

    def bias(shape, s=0.02):
        return s * jax.random.normal(next(ks), shape, f32)

    x = jax.random.normal(next(ks), (BATCH, SEQ, D_MODEL), f32)
    offset = jax.random.randint(next(ks), (BATCH, 1), 0, 1024, dtype=jnp.int32)
    positions = offset + jnp.arange(SEQ, dtype=jnp.int32)[None, :]

    def ffn(prefix, d):
        d[prefix + '_norm'] = gain((D_MODEL,))
        d[prefix + '_w_gate'] = dense((D_MODEL, D_FF), D_MODEL)
        d[prefix + '_w_up'] = dense((D_MODEL, D_FF), D_MODEL)
        d[prefix + '_w_down'] = dense((D_FF, D_MODEL), D_FF)

    d = {'x': x, 'positions': positions}
    ffn('l0_ffn1', d)
    d['l0_mix_norm'] = gain((D_MODEL,))
    d['l0_w_in'] = dense((D_MODEL, AB_IN), D_MODEL)
    d['l0_conv_w'] = dense((CONV_WIDTH, 1, CONV_CH), CONV_WIDTH)
    d['l0_conv_b'] = bias((CONV_CH,))
    d['l0_conv_ln_g'] = gain((CONV_CH,))
    d['l0_conv_ln_b'] = bias((CONV_CH,))
    d['l0_gmlp_ln_g'] = gain((GMLP_GROUPS, GMLP_GDIM))
    d['l0_gmlp_ln_b'] = bias((GMLP_GROUPS, GMLP_GDIM))
    d['l0_gmlp_ws'] = dense((GMLP_GROUPS, GMLP_CHUNK, GMLP_CHUNK), GMLP_CHUNK)
    d['l0_gmlp_bs'] = gain((GMLP_GROUPS, GMLP_CHUNK))
    d['l0_w_out'] = dense((CONV_CH + GMLP_CH, D_MODEL), CONV_CH + GMLP_CH)
    ffn('l0_ffn2', d)
    ffn('l1_ffn1', d)
    d['l1_mix_norm'] = gain((D_MODEL,))
    d['l1_w_in'] = dense((D_MODEL, NSA_IN), D_MODEL)
    d['l1_cmp_pe_k'] = bias((CMP_LEN, HEAD_DIM), 0.1)
    d['l1_cmp_w1_k'] = dense((CMP_LEN * HEAD_DIM, CMP_HIDDEN), CMP_LEN * HEAD_DIM)
    d['l1_cmp_w2_k'] = dense((CMP_HIDDEN, HEAD_DIM), CMP_HIDDEN)
    d['l1_cmp_pe_v'] = bias((CMP_LEN, HEAD_DIM), 0.1)
    d['l1_cmp_w1_v'] = dense((CMP_LEN * HEAD_DIM, CMP_HIDDEN), CMP_LEN * HEAD_DIM)
    d['l1_cmp_w2_v'] = dense((CMP_HIDDEN, HEAD_DIM), CMP_HIDDEN)
    d['l1_w_out'] = dense((D_MODEL, D_MODEL), D_MODEL)
    ffn('l1_ffn2', d)
    d['final_norm'] = gain((D_MODEL,))
    return d


def reference(x, positions,
              l0_ffn1_norm, l0_ffn1_w_gate, l0_ffn1_w_up, l0_ffn1_w_down,
              l0_mix_norm, l0_w_in, l0_conv_w, l0_conv_b, l0_conv_ln_g, l0_conv_ln_b,
              l0_gmlp_ln_g, l0_gmlp_ln_b, l0_gmlp_ws, l0_gmlp_bs, l0_w_out,
              l0_ffn2_norm, l0_ffn2_w_gate, l0_ffn2_w_up, l0_ffn2_w_down,
              l1_ffn1_norm, l1_ffn1_w_gate, l1_ffn1_w_up, l1_ffn1_w_down,
              l1_mix_norm, l1_w_in, l1_cmp_pe_k, l1_cmp_w1_k, l1_cmp_w2_k,
              l1_cmp_pe_v, l1_cmp_w1_v, l1_cmp_w2_v, l1_w_out,
              l1_ffn2_norm, l1_ffn2_w_gate, l1_ffn2_w_up, l1_ffn2_w_down,
              final_norm):
    cos, sin = rope_tables(positions)
    layers = (
        dict(ffn1=(l0_ffn1_norm, l0_ffn1_w_gate, l0_ffn1_w_up, l0_ffn1_w_down),
             mix_norm=l0_mix_norm,
             mix=(l0_w_in, l0_conv_w, l0_conv_b, l0_conv_ln_g, l0_conv_ln_b,
                  l0_gmlp_ln_g, l0_gmlp_ln_b, l0_gmlp_ws, l0_gmlp_bs, l0_w_out),
             ffn2=(l0_ffn2_norm, l0_ffn2_w_gate, l0_ffn2_w_up, l0_ffn2_w_down)),
        dict(ffn1=(l1_ffn1_norm, l1_ffn1_w_gate, l1_ffn1_w_up, l1_ffn1_w_down),
             mix_norm=l1_mix_norm,
             mix=(l1_w_in, l1_cmp_pe_k, l1_cmp_w1_k, l1_cmp_w2_k,
                  l1_cmp_pe_v, l1_cmp_w1_v, l1_cmp_w2_v, l1_w_out),
             ffn2=(l1_ffn2_norm, l1_ffn2_w_gate, l1_ffn2_w_up, l1_ffn2_w_down)),
    )
    for i in range(DEPTH):
        L = layers[i]
        n1, g1, u1, d1 = L['ffn1']
        x = x + 0.5 * swiglu(rmsnorm(x, n1), g1, u1, d1)
        hn = rmsnorm(x, L['mix_norm'])
        if i % 2 == 0:
            x = x + conv_gmlp_mixer(hn, *L['mix'])
        else:
            x = x + nsa_mixer(hn, cos, sin, *L['mix'])
        n2, g2, u2, d2 = L['ffn2']
        x = x + 0.5 * swiglu(rmsnorm(x, n2), g2, u2, d2)
    return rmsnorm(x, final_norm)
```

```python
import functools

import numpy as np
import jax
import jax.numpy as jnp
from jax import lax
from jax.experimental import pallas as pl
from jax.experimental.pallas import tpu as pltpu

F32 = jnp.float32
BF16 = jnp.bfloat16

D_MODEL = 2048
D_FF = 5632
EPS = 1e-6
CONV_CH = 1024
CONV_WIDTH = 31
GMLP_CH = 1024
GMLP_GROUPS = 8
GMLP_GDIM = 128
GMLP_CHUNK = 128
HEAD_DIM = 128
N_HEADS = 16
N_KV = 4
HPG = 4
KV_W = N_KV * HEAD_DIM
ROT_DIM = 32
ROPE_THETA = 500000.0
CMP_LEN = 32
CMP_STRIDE = 16
CMP_HIDDEN = 256
SEL_BLOCK = 64
SEL_TOPK = 16
WINDOW = 512
FORCE_BONUS = 1e3
NEG = -1e30
NSA_MAIN = D_MODEL + 6 * KV_W
NSA_GATES = 3 * N_HEADS
NSA_PAD = 5632
SCALE = HEAD_DIM ** -0.5

MIB = 1024 * 1024


def _cparams(semantics, vmem_mib):
    return pltpu.CompilerParams(dimension_semantics=semantics,
                                vmem_limit_bytes=vmem_mib * MIB)


def _rms_scale(x):
    return x * lax.rsqrt(jnp.mean(x * x, axis=-1, keepdims=True) + EPS)


def _ffn_kernel(x_ref, nw_ref, wg_ref, wu_ref, wd_ref, *rest, final):
    if final:
        fn_ref, o_ref, h_scr = rest
    else:
        o_ref, h_scr = rest
    f = pl.program_id(1)

    @pl.when(f == 0)
    def _():
        h_scr[...] = (_rms_scale(x_ref[...]) * nw_ref[...]).astype(BF16)

    h = h_scr[...]
    g = jnp.dot(h, wg_ref[...], preferred_element_type=F32)
    u = jnp.dot(h, wu_ref[...], preferred_element_type=F32)
    a = (g * jax.nn.sigmoid(g) * u).astype(BF16)
    part = jnp.dot(a, wd_ref[...], preferred_element_type=F32)

    @pl.when(f == 0)
    def _():
        o_ref[...] = part

    @pl.when(f > 0)
    def _():
        o_ref[...] += part

    @pl.when(f == pl.num_programs(1) - 1)
    def _():
        y = x_ref[...] + 0.5 * o_ref[...]
        if final:
            y = _rms_scale(y) * fn_ref[...]
        o_ref[...] = y


def _ffn(x2d, norm_w, wg, wu, wd, final_w=None, tm=512, tf=512):
    m, d = x2d.shape
    dff = wg.shape[1]
    final = final_w is not None
    in_specs = [
        pl.BlockSpec((tm, d), lambda i, j: (i, 0)),
        pl.BlockSpec((1, d), lambda i, j: (0, 0)),
        pl.BlockSpec((d, tf), lambda i, j: (0, j)),
        pl.BlockSpec((d, tf), lambda i, j: (0, j)),
        pl.BlockSpec((tf, d), lambda i, j: (j, 0)),
    ]
    args = [x2d, norm_w.reshape(1, d), wg, wu, wd]
    if final:
        in_specs.append(pl.BlockSpec((1, d), lambda i, j: (0, 0)))
        args.append(final_w.reshape(1, d))
    return pl.pallas_call(
        functools.partial(_ffn_kernel, final=final),
        grid=(m // tm, dff // tf),
        in_specs=in_specs,
        out_specs=pl.BlockSpec((tm, d), lambda i, j: (i, 0)),
        out_shape=jax.ShapeDtypeStruct((m, d), F32),
        scratch_shapes=[pltpu.VMEM((tm, d), BF16)],
        compiler_params=_cparams(("parallel", "arbitrary"), 48),
        name="ffn",
    )(*args)


def _norm_matmul_kernel(x_ref, nw_ref, w_ref, o_ref, h_scr):
    @pl.when(pl.program_id(1) == 0)
    def _():
        h_scr[...] = (_rms_scale(x_ref[...]) * nw_ref[...]).astype(BF16)

    o_ref[...] = jnp.dot(h_scr[...], w_ref[...], preferred_element_type=F32)


def _norm_matmul(x2d, norm_w, w, tm=512, tn=512):
    m, d = x2d.shape
    n = w.shape[1]
    return pl.pallas_call(
        _norm_matmul_kernel,
        grid=(m // tm, n // tn),
        in_specs=[
            pl.BlockSpec((tm, d), lambda i, j: (i, 0)),
            pl.BlockSpec((1, d), lambda i, j: (0, 0)),
            pl.BlockSpec((d, tn), lambda i, j: (0, j)),
        ],
        out_specs=pl.BlockSpec((tm, tn), lambda i, j: (i, j)),
        out_shape=jax.ShapeDtypeStruct((m, n), F32),
        scratch_shapes=[pltpu.VMEM((tm, d), BF16)],
        compiler_params=_cparams(("parallel", "arbitrary"), 32),
        name="norm_matmul",
    )(x2d, norm_w.reshape(1, d), w)


def _matmul_res_kernel(a_ref, w_ref, r_ref, o_ref):
    o_ref[...] = r_ref[...] + jnp.dot(a_ref[...], w_ref[...], preferred_element_type=F32)


def _matmul_res(a, w, res, tm=512, tn=1024):
    m, k = a.shape
    n = w.shape[1]
    return pl.pallas_call(
        _matmul_res_kernel,
        grid=(m // tm, n // tn),
        in_specs=[
            pl.BlockSpec((tm, k), lambda i, j: (i, 0)),
            pl.BlockSpec((k, tn), lambda i, j: (0, j)),
            pl.BlockSpec((tm, tn), lambda i, j: (i, j)),
        ],
        out_specs=pl.BlockSpec((tm, tn), lambda i, j: (i, j)),
        out_shape=jax.ShapeDtypeStruct((m, n), F32),
        compiler_params=_cparams(("parallel", "arbitrary"), 32),
        name="matmul_res",
    )(a, w, res)


CONV_HALO = 32


def _layernorm(x, g, b):
    mu = jnp.mean(x, axis=-1, keepdims=True)
    xc = x - mu
    var = jnp.mean(xc * xc, axis=-1, keepdims=True)
    return xc * lax.rsqrt(var + EPS) * g + b


def _l0_mid_kernel(av_ref, ag_ref, bu_ref, bv_ref, cw_ref, cb_ref, clg_ref, clb_ref,
                   glg_ref, glb_ref, ws_ref, bst_ref, o_ref, abuf, *, ts):
    s = pl.program_id(1)

    @pl.when(s == 0)
    def _():
        abuf[0:CONV_HALO, :] = jnp.zeros((CONV_HALO, CONV_CH), F32)

    abuf[CONV_HALO:CONV_HALO + ts, :] = av_ref[0] * jax.nn.sigmoid(ag_ref[0])
    base = CONV_HALO - (CONV_WIDTH - 1)
    acc = jnp.zeros((ts, CONV_CH), F32)
    for j in range(CONV_WIDTH):
        acc = acc + abuf[base + j:base + j + ts, :] * cw_ref[j:j + 1, :]
    acc = acc + cb_ref[...]
    a = _layernorm(acc, clg_ref[...], clb_ref[...])
    o_ref[0, :, 0:CONV_CH] = (a * jax.nn.sigmoid(a)).astype(o_ref.dtype)
    abuf[0:CONV_HALO, :] = abuf[ts:ts + CONV_HALO, :]

    row = lax.broadcasted_iota(jnp.int32, (GMLP_CHUNK, GMLP_CHUNK), 0)
    col = lax.broadcasted_iota(jnp.int32, (GMLP_CHUNK, GMLP_CHUNK), 1)
    tri = col <= row
    for g in range(GMLP_GROUPS):
        lo = g * GMLP_GDIM
        w_g = jnp.where(tri, ws_ref[g], 0.0).astype(BF16)
        bias_g = bst_ref[:, g:g + 1]
        ln_g = glg_ref[:, lo:lo + GMLP_GDIM]
        ln_b = glb_ref[:, lo:lo + GMLP_GDIM]
        for c in range(ts // GMLP_CHUNK):
            r0 = c * GMLP_CHUNK
            u = jax.nn.gelu(bu_ref[0, r0:r0 + GMLP_CHUNK, lo:lo + GMLP_GDIM])
            v = jax.nn.gelu(bv_ref[0, r0:r0 + GMLP_CHUNK, lo:lo + GMLP_GDIM])
            v = _layernorm(v, ln_g, ln_b)
            sv = jnp.dot(w_g, v.astype(BF16), preferred_element_type=F32) + bias_g
            o_ref[0, r0:r0 + GMLP_CHUNK, CONV_CH + lo:CONV_CH + lo + GMLP_GDIM] = (
                (u * sv).astype(o_ref.dtype))


def _l0_mid(z, conv_w, conv_b, conv_ln_g, conv_ln_b, gmlp_ln_g, gmlp_ln_b, gmlp_ws, gmlp_bs, ts=256):
    b, s, _ = z.shape
    vec = lambda a: a.reshape(1, -1)
    const2 = lambda shape: pl.BlockSpec(shape, lambda i, j: (0, 0))
    zspec = lambda k: pl.BlockSpec((1, ts, CONV_CH), lambda i, j, k=k: (i, j, k))
    return pl.pallas_call(
        functools.partial(_l0_mid_kernel, ts=ts),
        grid=(b, s // ts),
        in_specs=[
            zspec(0), zspec(1), zspec(2), zspec(3),
            const2((CONV_WIDTH, CONV_CH)),
            const2((1, CONV_CH)), const2((1, CONV_CH)), const2((1, CONV_CH)),
            const2((1, GMLP_CH)), const2((1, GMLP_CH)),
            pl.BlockSpec((GMLP_GROUPS, GMLP_CHUNK, GMLP_CHUNK), lambda i, j: (0, 0, 0)),
            const2((GMLP_CHUNK, GMLP_GROUPS)),
        ],
        out_specs=pl.BlockSpec((1, ts, CONV_CH + GMLP_CH), lambda i, j: (i, j, 0)),
        out_shape=jax.ShapeDtypeStruct((b, s, CONV_CH + GMLP_CH), BF16),
        scratch_shapes=[pltpu.VMEM((CONV_HALO + ts, CONV_CH), F32)],
        compiler_params=_cparams(("parallel", "arbitrary"), 32),
        name="l0_mid",
    )(z, z, z, z, conv_w.reshape(CONV_WIDTH, CONV_CH), vec(conv_b), vec(conv_ln_g), vec(conv_ln_b),
      vec(gmlp_ln_g), vec(gmlp_ln_b), gmlp_ws, gmlp_bs.T)


SEL_TK = 512
WIN_TK = 128
N_HALF = 128


def _rope(x, c, a, b):
    return x * c + pltpu.roll(x, HEAD_DIM - ROT_DIM // 2, 1) * a + pltpu.roll(x, ROT_DIM // 2, 1) * b


def _compress(k_ref, pe_ref, w1_ref, w2_ref):
    half = CMP_LEN // 2
    p = jnp.zeros((N_HALF, CMP_HIDDEN), F32)
    q = jnp.zeros((N_HALF, CMP_HIDDEN), F32)
    for l in range(half):
        rows = k_ref[0, pl.ds(l, N_HALF, stride=CMP_STRIDE), :]
        p = p + jnp.dot((rows + pe_ref[l:l + 1, :]).astype(BF16), w1_ref[l],
                        preferred_element_type=F32)
        q = q + jnp.dot((rows + pe_ref[half + l:half + l + 1, :]).astype(BF16), w1_ref[half + l],
                        preferred_element_type=F32)
    hid = p + pltpu.roll(q, N_HALF - 1, 0)
    return jnp.dot(jax.nn.gelu(hid).astype(BF16), w2_ref[...], preferred_element_type=F32)


def _nsa_prep_kernel(kc_ref, vc_ref, ks_ref, vs_ref, kw_ref, vw_ref, rc_ref, ra_ref, rb_ref,
                     pek_ref, w1k_ref, w2k_ref, pev_ref, w1v_ref, w2v_ref,
                     ksr_ref, vst_ref, kwr_ref, vwt_ref, kcmp_ref, vcmpt_ref):
    c, a, b = rc_ref[0], ra_ref[0], rb_ref[0]
    ksr_ref[0, 0] = _rope(ks_ref[0], c, a, b).astype(BF16)
    kwr_ref[0, 0] = _rope(kw_ref[0], c, a, b).astype(BF16)
    s = vs_ref.shape[1]
    for t in range(s // SEL_TK):
        vst_ref[0, 0, t] = vs_ref[0, t * SEL_TK:(t + 1) * SEL_TK, :].T.astype(BF16)
    for t in range(s // WIN_TK):
        vwt_ref[0, 0, t] = vw_ref[0, t * WIN_TK:(t + 1) * WIN_TK, :].T.astype(BF16)
    kcmp_ref[0, 0] = _compress(kc_ref, pek_ref, w1k_ref, w2k_ref).astype(BF16)
    vcmpt_ref[0, 0] = _compress(vc_ref, pev_ref, w1v_ref, w2v_ref).T.astype(BF16)


def _nsa_prep(z, rope_c, rope_a, rope_b, pe_k, w1_k, w2_k, pe_v, w1_v, w2_v):
    b, s, _ = z.shape
    first = D_MODEL // HEAD_DIM
    zspec = lambda k: pl.BlockSpec((1, s, HEAD_DIM), lambda i, g, k=k: (i, 0, first + k * N_KV + g))
    tab = pl.BlockSpec((1, s, HEAD_DIM), lambda i, g: (i, 0, 0))
    c2 = lambda shape: pl.BlockSpec(shape, lambda i, g: (0, 0))
    c3 = lambda shape: pl.BlockSpec(shape, lambda i, g: (0, 0, 0))
    out4 = lambda shape: pl.BlockSpec((1, 1) + shape, lambda i, g: (i, g, 0, 0))
    out5 = lambda shape: pl.BlockSpec((1, 1) + shape, lambda i, g: (i, g, 0, 0, 0))
    w1 = lambda w: w.reshape(CMP_LEN, HEAD_DIM, CMP_HIDDEN)
    return pl.pallas_call(
        _nsa_prep_kernel,
        grid=(b, N_KV),
        in_specs=[zspec(0), zspec(1), zspec(2), zspec(3), zspec(4), zspec(5), tab, tab, tab,
                  c2((CMP_LEN, HEAD_DIM)), c3((CMP_LEN, HEAD_DIM, CMP_HIDDEN)), c2((CMP_HIDDEN, HEAD_DIM)),
                  c2((CMP_LEN, HEAD_DIM)), c3((CMP_LEN, HEAD_DIM, CMP_HIDDEN)), c2((CMP_HIDDEN, HEAD_DIM))],
        out_specs=[out4((s, HEAD_DIM)), out5((s // SEL_TK, HEAD_DIM, SEL_TK)),
                   out4((s, HEAD_DIM)), out5((s // WIN_TK, HEAD_DIM, WIN_TK)),
                   out4((N_HALF, HEAD_DIM)), out4((HEAD_DIM, N_HALF))],
        out_shape=[jax.ShapeDtypeStruct((b, N_KV, s, HEAD_DIM), BF16),
                   jax.ShapeDtypeStruct((b, N_KV, s // SEL_TK, HEAD_DIM, SEL_TK), BF16),
                   jax.ShapeDtypeStruct((b, N_KV, s, HEAD_DIM), BF16),
                   jax.ShapeDtypeStruct((b, N_KV, s // WIN_TK, HEAD_DIM, WIN_TK), BF16),
                   jax.ShapeDtypeStruct((b, N_KV, N_HALF, HEAD_DIM), BF16),
                   jax.ShapeDtypeStruct((b, N_KV, HEAD_DIM, N_HALF), BF16)],
        compiler_params=_cparams(("parallel", "arbitrary"), 40),
        name="nsa_prep",
    )(z, z, z, z, z, z, rope_c, rope_a, rope_b,
      pe_k, w1(w1_k).astype(BF16), w2_k.astype(BF16), pe_v, w1(w1_v).astype(BF16), w2_v.astype(BF16))


def _online_attend(q_all, k_ref, vt_ref, tile, lo, hi, bias_fn, rows):
    def body(kt, carry):
        m, l, acc = carry
        k = k_ref[0, 0, pl.ds(pl.multiple_of(kt * tile, tile), tile), :]
        s = lax.dot_general(k, q_all, (((1,), (1,)), ((), ())), preferred_element_type=F32) * SCALE
        s = s + bias_fn(kt)
        m_new = jnp.maximum(m, jnp.max(s, axis=0, keepdims=True))
        alpha = jnp.exp(m - m_new)
        p = jnp.exp(s - m_new)
        l = alpha * l + jnp.sum(p, axis=0, keepdims=True)
        acc = alpha * acc + jnp.dot(vt_ref[0, 0, kt], p.astype(BF16), preferred_element_type=F32)
        return m_new, l, acc

    init = (jnp.full((1, rows), NEG, F32), jnp.zeros((1, rows), F32), jnp.zeros((HEAD_DIM, rows), F32))
    m, l, acc = lax.fori_loop(lo, hi, body, init)
    return acc / l


def _nsa_attn_kernel(q_ref, zg_ref, rc_ref, ra_ref, rb_ref, ksr_ref, vst_ref, kwr_ref, vwt_ref,
                     kcmp_ref, vcmpt_ref, ovt_ref, exp_ref, o_ref, bias_scr, gate_scr, *, tq, seq):
    grp = pl.program_id(1)
    t0 = pl.program_id(2) * tq
    rows = HPG * tq
    heads = lambda x: jnp.concatenate([x] * HPG, axis=1)

    c, a, b = rc_ref[0], ra_ref[0], rb_ref[0]
    q_plain, q_rot = [], []
    for h in range(HPG):
        qh = q_ref[0, :, h * HEAD_DIM:(h + 1) * HEAD_DIM]
        q_plain.append(qh.astype(BF16))
        q_rot.append(_rope(qh, c, a, b).astype(BF16))
    q_plain = jnp.concatenate(q_plain, axis=0)
    q_rot = jnp.concatenate(q_rot, axis=0)

    tpos = t0 + lax.broadcasted_iota(jnp.int32, (1, tq), 1)

    s_c = lax.dot_general(kcmp_ref[0, 0], q_plain, (((1,), (1,)), ((), ())),
                          preferred_element_type=F32) * SCALE
    n_idx = lax.broadcasted_iota(jnp.int32, (N_HALF, rows), 0)
    cmask = n_idx * CMP_STRIDE + (CMP_LEN - 1) <= heads(tpos)
    s_c = jnp.where(cmask, s_c, NEG)
    e = jnp.exp(s_c - jnp.max(s_c, axis=0, keepdims=True))
    p_c = jnp.where(cmask, e / jnp.sum(e, axis=0, keepdims=True), 0.0)
    o_c = jnp.dot(vcmpt_ref[0, 0], p_c.astype(BF16), preferred_element_type=F32)

    p_sum = p_c[:, 0:tq]
    for h in range(1, HPG):
        p_sum = p_sum + p_c[:, h * tq:(h + 1) * tq]
    p_hi = p_sum.astype(BF16)
    p_lo = (p_sum - p_hi.astype(F32)).astype(BF16)
    ovt = ovt_ref[...]
    imp = (jnp.dot(ovt, p_hi, preferred_element_type=F32)
           + jnp.dot(ovt, p_lo, preferred_element_type=F32))
    n_blk = seq // SEL_BLOCK
    j_idx = lax.broadcasted_iota(jnp.int32, (n_blk, tq), 0)
    cur = tpos // SEL_BLOCK
    forced = (j_idx == 0) | (j_idx == cur) | (j_idx == cur - 1)
    imp = imp + jnp.where(forced, FORCE_BONUS, 0.0)
    imp = jnp.where(j_idx * SEL_BLOCK <= tpos, imp, NEG)
    rank = jnp.zeros((n_blk, tq), jnp.int32)
    for i in range(n_blk):
        vi = imp[i:i + 1, :]
        ahead = (vi > imp) | ((vi == imp) & (j_idx > i))
        rank = rank + ahead.astype(jnp.int32)
    sel = (rank < min(SEL_TOPK, n_blk)).astype(BF16)

    sel_keys = jnp.dot(exp_ref[...], sel, preferred_element_type=F32)
    kpos = lax.broadcasted_iota(jnp.int32, (seq, tq), 0)
    bias_scr[...] = jnp.where((sel_keys > 0.5) & (kpos <= tpos), 0.0, NEG)

    def sel_bias(kt):
        return heads(bias_scr[pl.ds(pl.multiple_of(kt * SEL_TK, SEL_TK), SEL_TK), :])

    o_s = _online_attend(q_rot, ksr_ref, vst_ref, SEL_TK, 0, (t0 + tq + SEL_TK - 1) // SEL_TK,
                         sel_bias, rows)

    def win_bias(kt):
        kp = kt * WIN_TK + lax.broadcasted_iota(jnp.int32, (WIN_TK, tq), 0)
        return heads(jnp.where((kp <= tpos) & (kp > tpos - WINDOW), 0.0, NEG))

    o_w = _online_attend(q_rot, kwr_ref, vwt_ref, WIN_TK, jnp.maximum(t0 - WINDOW, 0) // WIN_TK,
                         (t0 + tq) // WIN_TK, win_bias, rows)

    gate_scr[...] = jax.nn.sigmoid(zg_ref[0]).T

    def gate(branch):
        return jnp.concatenate(
            [gate_scr[pl.ds(3 * (HPG * grp + h) + branch, 1), :] for h in range(HPG)], axis=1)

    o = gate(0) * o_c + gate(1) * o_s + gate(2) * o_w
    for h in range(HPG):
        o_ref[0, :, h * HEAD_DIM:(h + 1) * HEAD_DIM] = o[:, h * tq:(h + 1) * tq].T.astype(o_ref.dtype)


def _nsa_attn(z, rope_c, rope_a, rope_b, ksr, vst, kwr, vwt, kcmp, vcmpt, tq=128):
    b, s, _ = z.shape
    n_blk = s // SEL_BLOCK
    n = np.arange(N_HALF)[None, :]
    j = np.arange(n_blk)[:, None]
    ovt = ((n < N_HALF - 1) & (n * CMP_STRIDE < (j + 1) * SEL_BLOCK)
           & (n * CMP_STRIDE + CMP_LEN - 1 >= j * SEL_BLOCK))
    expand = (np.arange(s)[:, None] // SEL_BLOCK) == np.arange(n_blk)[None, :]
    gate_blk = NSA_MAIN // HEAD_DIM
    tab = pl.BlockSpec((1, tq, HEAD_DIM), lambda i, g, t: (i, t, 0))
    kv4 = lambda shape: pl.BlockSpec((1, 1) + shape, lambda i, g, t: (i, g, 0, 0))
    kv5 = lambda shape: pl.BlockSpec((1, 1) + shape, lambda i, g, t: (i, g, 0, 0, 0))
    return pl.pallas_call(
        functools.partial(_nsa_attn_kernel, tq=tq, seq=s),
        grid=(b, N_KV, s // tq),
        in_specs=[
            pl.BlockSpec((1, tq, HPG * HEAD_DIM), lambda i, g, t: (i, t, g)),
            pl.BlockSpec((1, tq, HEAD_DIM), lambda i, g, t: (i, t, gate_blk)),
            tab, tab, tab,
            kv4((s, HEAD_DIM)), kv5((s // SEL_TK, HEAD_DIM, SEL_TK)),
            kv4((s, HEAD_DIM)), kv5((s // WIN_TK, HEAD_DIM, WIN_TK)),
            kv4((N_HALF, HEAD_DIM)), kv4((HEAD_DIM, N_HALF)),
            pl.BlockSpec((n_blk, N_HALF), lambda i, g, t: (0, 0)),
            pl.BlockSpec((s, n_blk), lambda i, g, t: (0, 0)),
        ],
        out_specs=pl.BlockSpec((1, tq, HPG * HEAD_DIM), lambda i, g, t: (i, t, g)),
        out_shape=jax.ShapeDtypeStruct((b, s, D_MODEL), BF16),
        scratch_shapes=[pltpu.VMEM((s, tq), F32), pltpu.VMEM((HEAD_DIM, tq), F32)],
        compiler_params=_cparams(("parallel", "parallel", "arbitrary"), 40),
        name="nsa_attn",
    )(z, z, rope_c, rope_a, rope_b, ksr, vst, kwr, vwt, kcmp, vcmpt,
      jnp.asarray(ovt, BF16), jnp.asarray(expand, BF16))


def _rope_tables(positions):
    half = ROT_DIM // 2
    inv = ROPE_THETA ** (-jnp.arange(0, ROT_DIM, 2, dtype=F32) / ROT_DIM)
    ang = positions.astype(F32)[..., None] * inv
    cos, sin = jnp.cos(ang), jnp.sin(ang)
    zeros = lambda w: jnp.zeros(cos.shape[:-1] + (w,), F32)
    c = jnp.concatenate([cos, cos, jnp.ones(cos.shape[:-1] + (HEAD_DIM - ROT_DIM,), F32)], axis=-1)
    a = jnp.concatenate([-sin, zeros(HEAD_DIM - half)], axis=-1)
    b = jnp.concatenate([zeros(half), sin, zeros(HEAD_DIM - ROT_DIM)], axis=-1)
    return c, a, b


def kernel(x, positions, l0_ffn1_norm, l0_ffn1_w_gate, l0_ffn1_w_up, l0_ffn1_w_down, l0_mix_norm, l0_w_in, l0_conv_w, l0_conv_b, l0_conv_ln_g, l0_conv_ln_b, l0_gmlp_ln_g, l0_gmlp_ln_b, l0_gmlp_ws, l0_gmlp_bs, l0_w_out, l0_ffn2_norm, l0_ffn2_w_gate, l0_ffn2_w_up, l0_ffn2_w_down, l1_ffn1_norm, l1_ffn1_w_gate, l1_ffn1_w_up, l1_ffn1_w_down, l1_mix_norm, l1_w_in, l1_cmp_pe_k, l1_cmp_w1_k, l1_cmp_w2_k, l1_cmp_pe_v, l1_cmp_w1_v, l1_cmp_w2_v, l1_w_out, l1_ffn2_norm, l1_ffn2_w_gate, l1_ffn2_w_up, l1_ffn2_w_down, final_norm):
    b, s, d = x.shape
    m = b * s
    bf = lambda w: w.astype(BF16)
    x2 = x.reshape(m, d)

    x2 = _ffn(x2, l0_ffn1_norm, bf(l0_ffn1_w_gate), bf(l0_ffn1_w_up), bf(l0_ffn1_w_down))
    z = _norm_matmul(x2, l0_mix_norm, bf(l0_w_in)).reshape(b, s, -1)
    cat = _l0_mid(z, l0_conv_w, l0_conv_b, l0_conv_ln_g, l0_conv_ln_b,
                  l0_gmlp_ln_g, l0_gmlp_ln_b, l0_gmlp_ws, l0_gmlp_bs)
    x2 = _matmul_res(cat.reshape(m, -1), bf(l0_w_out), x2)
    x2 = _ffn(x2, l0_ffn2_norm, bf(l0_ffn2_w_gate), bf(l0_ffn2_w_up), bf(l0_ffn2_w_down))

    x2 = _ffn(x2, l1_ffn1_norm, bf(l1_ffn1_w_gate), bf(l1_ffn1_w_up), bf(l1_ffn1_w_down))
    w_in = jnp.pad(bf(l1_w_in), ((0, 0), (0, NSA_PAD - NSA_MAIN - NSA_GATES)))
    z = _norm_matmul(x2, l1_mix_norm, w_in).reshape(b, s, -1)
    rope_c, rope_a, rope_b = _rope_tables(positions)
    ksr, vst, kwr, vwt, kcmp, vcmpt = _nsa_prep(
        z, rope_c, rope_a, rope_b, l1_cmp_pe_k, l1_cmp_w1_k, l1_cmp_w2_k,
        l1_cmp_pe_v, l1_cmp_w1_v, l1_cmp_w2_v)
    o = _nsa_attn(z, rope_c, rope_a, rope_b, ksr, vst, kwr, vwt, kcmp, vcmpt)
    x2 = _matmul_res(o.reshape(m, -1), bf(l1_w_out), x2)
    x2 = _ffn(x2, l1_ffn2_norm, bf(l1_ffn2_w_gate), bf(l1_ffn2_w_up), bf(l1_ffn2_w_down),
              final_w=final_norm)
    return x2.reshape(b, s, d)
```

```python
import functools

import numpy as np
import jax
import jax.numpy as jnp
from jax import lax
from jax.experimental import pallas as pl
from jax.experimental.pallas import tpu as pltpu

F32 = jnp.float32
BF16 = jnp.bfloat16

D_MODEL = 2048
D_FF = 5632
EPS = 1e-6
CONV_CH = 1024
CONV_WIDTH = 31
GMLP_CH = 1024
GMLP_GROUPS = 8
GMLP_GDIM = 128
GMLP_CHUNK = 128
HEAD_DIM = 128
N_HEADS = 16
N_KV = 4
HPG = 4
KV_W = N_KV * HEAD_DIM
ROT_DIM = 32
ROPE_THETA = 500000.0
CMP_LEN = 32
CMP_STRIDE = 16
CMP_HIDDEN = 256
SEL_BLOCK = 64
SEL_TOPK = 16
WINDOW = 512
FORCE_BONUS = 1e3
NEG = -1e30
NSA_MAIN = D_MODEL + 6 * KV_W
NSA_GATES = 3 * N_HEADS
NSA_PAD = 5632
SCALE = HEAD_DIM ** -0.5

MIB = 1024 * 1024


def _cparams(semantics, vmem_mib):
    return pltpu.CompilerParams(dimension_semantics=semantics,
                                vmem_limit_bytes=vmem_mib * MIB)


def _rms_scale(x):
    return x * lax.rsqrt(jnp.mean(x * x, axis=-1, keepdims=True) + EPS)


def _ffn_kernel(x_ref, nw_ref, wg_ref, wu_ref, wd_ref, *rest, final):
    if final:
        fn_ref, o_ref, h_scr = rest
    else:
        o_ref, h_scr = rest
    f = pl.program_id(1)

    @pl.when(f == 0)
    def _():
        h_scr[...] = (_rms_scale(x_ref[...]) * nw_ref[...]).astype(BF16)
        o_ref[...] = jnp.zeros(o_ref.shape, F32)

    h = h_scr[...]
    g = jnp.dot(h, wg_ref[...], preferred_element_type=F32)
    u = jnp.dot(h, wu_ref[...], preferred_element_type=F32)
    a = (g * jax.nn.sigmoid(g) * u).astype(BF16)
    o_ref[...] += jnp.dot(a, wd_ref[...], preferred_element_type=F32)

    @pl.when(f == pl.num_programs(1) - 1)
    def _():
        y = x_ref[...] + 0.5 * o_ref[...]
        if final:
            y = _rms_scale(y) * fn_ref[...]
        o_ref[...] = y


def _ffn(x2d, norm_w, wg, wu, wd, final_w=None, tm=512, tf=512):
    m, d = x2d.shape
    dff = wg.shape[1]
    final = final_w is not None
    in_specs = [
        pl.BlockSpec((tm, d), lambda i, j: (i, 0)),
        pl.BlockSpec((1, d), lambda i, j: (0, 0)),
        pl.BlockSpec((d, tf), lambda i, j: (0, j)),
        pl.BlockSpec((d, tf), lambda i, j: (0, j)),
        pl.BlockSpec((tf, d), lambda i, j: (j, 0)),
    ]
    args = [x2d, norm_w.reshape(1, d), wg, wu, wd]
    if final:
        in_specs.append(pl.BlockSpec((1, d), lambda i, j: (0, 0)))
        args.append(final_w.reshape(1, d))
    return pl.pallas_call(
        functools.partial(_ffn_kernel, final=final),
        grid=(m // tm, dff // tf),
        in_specs=in_specs,
        out_specs=pl.BlockSpec((tm, d), lambda i, j: (i, 0)),
        out_shape=jax.ShapeDtypeStruct((m, d), F32),
        scratch_shapes=[pltpu.VMEM((tm, d), BF16)],
        compiler_params=_cparams(("parallel", "arbitrary"), 48),
        name="ffn",
    )(*args)


def _norm_matmul_kernel(x_ref, nw_ref, w_ref, o_ref, h_scr):
    @pl.when(pl.program_id(1) == 0)
    def _():
        h_scr[...] = (_rms_scale(x_ref[...]) * nw_ref[...]).astype(BF16)

    o_ref[...] = jnp.dot(h_scr[...], w_ref[...], preferred_element_type=F32)


def _norm_matmul(x2d, norm_w, w, tm=1024, tn=512):
    m, d = x2d.shape
    n = w.shape[1]
    return pl.pallas_call(
        _norm_matmul_kernel,
        grid=(m // tm, n // tn),
        in_specs=[
            pl.BlockSpec((tm, d), lambda i, j: (i, 0)),
            pl.BlockSpec((1, d), lambda i, j: (0, 0)),
            pl.BlockSpec((d, tn), lambda i, j: (0, j)),
        ],
        out_specs=pl.BlockSpec((tm, tn), lambda i, j: (i, j)),
        out_shape=jax.ShapeDtypeStruct((m, n), F32),
        scratch_shapes=[pltpu.VMEM((tm, d), BF16)],
        compiler_params=_cparams(("parallel", "arbitrary"), 40),
        name="norm_matmul",
    )(x2d, norm_w.reshape(1, d), w)


def _matmul_res_kernel(a_ref, w_ref, r_ref, o_ref):
    o_ref[...] = r_ref[...] + jnp.dot(a_ref[...], w_ref[...], preferred_element_type=F32)


def _matmul_res(a, w, res, tm=512, tn=1024):
    m, k = a.shape
    n = w.shape[1]
    return pl.pallas_call(
        _matmul_res_kernel,
        grid=(m // tm, n // tn),
        in_specs=[
            pl.BlockSpec((tm, k), lambda i, j: (i, 0)),
            pl.BlockSpec((k, tn), lambda i, j: (0, j)),
            pl.BlockSpec((tm, tn), lambda i, j: (i, j)),
        ],
        out_specs=pl.BlockSpec((tm, tn), lambda i, j: (i, j)),
        out_shape=jax.ShapeDtypeStruct((m, n), F32),
        compiler_params=_cparams(("parallel", "arbitrary"), 32),
        name="matmul_res",
    )(a, w, res)


CONV_HALO = 32


def _layernorm(x, g, b):
    mu = jnp.mean(x, axis=-1, keepdims=True)
    xc = x - mu
    var = jnp.mean(xc * xc, axis=-1, keepdims=True)
    return xc * lax.rsqrt(var + EPS) * g + b


def _l0_mid_kernel(av_ref, ag_ref, bu_ref, bv_ref, cw_ref, cb_ref, clg_ref, clb_ref,
                   glg_ref, glb_ref, ws_ref, bst_ref, o_ref, abuf, *, ts):
    s = pl.program_id(1)

    @pl.when(s == 0)
    def _():
        abuf[0:CONV_HALO, :] = jnp.zeros((CONV_HALO, CONV_CH), F32)

    abuf[CONV_HALO:CONV_HALO + ts, :] = av_ref[0] * jax.nn.sigmoid(ag_ref[0])
    base = CONV_HALO - (CONV_WIDTH - 1)
    acc = jnp.zeros((ts, CONV_CH), F32)
    for j in range(CONV_WIDTH):
        acc = acc + abuf[base + j:base + j + ts, :] * cw_ref[j:j + 1, :]
    acc = acc + cb_ref[...]
    a = _layernorm(acc, clg_ref[...], clb_ref[...])
    o_ref[0, :, 0:CONV_CH] = (a * jax.nn.sigmoid(a)).astype(o_ref.dtype)
    abuf[0:CONV_HALO, :] = abuf[ts:ts + CONV_HALO, :]

    row = lax.broadcasted_iota(jnp.int32, (GMLP_CHUNK, GMLP_CHUNK), 0)
    col = lax.broadcasted_iota(jnp.int32, (GMLP_CHUNK, GMLP_CHUNK), 1)
    tri = col <= row
    for g in range(GMLP_GROUPS):
        lo = g * GMLP_GDIM
        w_g = jnp.where(tri, ws_ref[g], 0.0).astype(BF16)
        bias_g = bst_ref[:, g:g + 1]
        ln_g = glg_ref[:, lo:lo + GMLP_GDIM]
        ln_b = glb_ref[:, lo:lo + GMLP_GDIM]
        for c in range(ts // GMLP_CHUNK):
            r0 = c * GMLP_CHUNK
            u = jax.nn.gelu(bu_ref[0, r0:r0 + GMLP_CHUNK, lo:lo + GMLP_GDIM])
            v = jax.nn.gelu(bv_ref[0, r0:r0 + GMLP_CHUNK, lo:lo + GMLP_GDIM])
            v = _layernorm(v, ln_g, ln_b)
            sv = jnp.dot(w_g, v.astype(BF16), preferred_element_type=F32) + bias_g
            o_ref[0, r0:r0 + GMLP_CHUNK, CONV_CH + lo:CONV_CH + lo + GMLP_GDIM] = (
                (u * sv).astype(o_ref.dtype))


def _l0_mid(z, conv_w, conv_b, conv_ln_g, conv_ln_b, gmlp_ln_g, gmlp_ln_b, gmlp_ws, gmlp_bs, ts=256):
    b, s, _ = z.shape
    vec = lambda a: a.reshape(1, -1)
    const2 = lambda shape: pl.BlockSpec(shape, lambda i, j: (0, 0))
    zspec = lambda k: pl.BlockSpec((1, ts, CONV_CH), lambda i, j, k=k: (i, j, k))
    return pl.pallas_call(
        functools.partial(_l0_mid_kernel, ts=ts),
        grid=(b, s // ts),
        in_specs=[
            zspec(0), zspec(1), zspec(2), zspec(3),
            const2((CONV_WIDTH, CONV_CH)),
            const2((1, CONV_CH)), const2((1, CONV_CH)), const2((1, CONV_CH)),
            const2((1, GMLP_CH)), const2((1, GMLP_CH)),
            pl.BlockSpec((GMLP_GROUPS, GMLP_CHUNK, GMLP_CHUNK), lambda i, j: (0, 0, 0)),
            const2((GMLP_CHUNK, GMLP_GROUPS)),
        ],
        out_specs=pl.BlockSpec((1, ts, CONV_CH + GMLP_CH), lambda i, j: (i, j, 0)),
        out_shape=jax.ShapeDtypeStruct((b, s, CONV_CH + GMLP_CH), BF16),
        scratch_shapes=[pltpu.VMEM((CONV_HALO + ts, CONV_CH), F32)],
        compiler_params=_cparams(("parallel", "arbitrary"), 32),
        name="l0_mid",
    )(z, z, z, z, conv_w.reshape(CONV_WIDTH, CONV_CH), vec(conv_b), vec(conv_ln_g), vec(conv_ln_b),
      vec(gmlp_ln_g), vec(gmlp_ln_b), gmlp_ws, gmlp_bs.T)


SEL_TK = 512
WIN_TK = 128
N_HALF = 128


def _rope(x, c, a, b):
    return x * c + pltpu.roll(x, HEAD_DIM - ROT_DIM // 2, 1) * a + pltpu.roll(x, ROT_DIM // 2, 1) * b


def _compress(k_ref, pe_ref, w1_ref, w2_ref):
    half = CMP_LEN // 2
    p = jnp.zeros((N_HALF, CMP_HIDDEN), F32)
    q = jnp.zeros((N_HALF, CMP_HIDDEN), F32)
    for l in range(half):
        rows = k_ref[0, pl.ds(l, N_HALF, stride=CMP_STRIDE), :]
        p = p + jnp.dot((rows + pe_ref[l:l + 1, :]).astype(BF16), w1_ref[l],
                        preferred_element_type=F32)
        q = q + jnp.dot((rows + pe_ref[half + l:half + l + 1, :]).astype(BF16), w1_ref[half + l],
                        preferred_element_type=F32)
    hid = p + pltpu.roll(q, N_HALF - 1, 0)
    return jnp.dot(jax.nn.gelu(hid).astype(BF16), w2_ref[...], preferred_element_type=F32)


def _nsa_prep_kernel(kc_ref, vc_ref, ks_ref, vs_ref, kw_ref, vw_ref, rc_ref, ra_ref, rb_ref,
                     pek_ref, w1k_ref, w2k_ref, pev_ref, w1v_ref, w2v_ref,
                     ksr_ref, vst_ref, kwr_ref, vwt_ref, kcmp_ref, vcmpt_ref):
    c, a, b = rc_ref[0], ra_ref[0], rb_ref[0]
    ksr_ref[0, 0] = _rope(ks_ref[0], c, a, b).astype(BF16)
    kwr_ref[0, 0] = _rope(kw_ref[0], c, a, b).astype(BF16)
    s = vs_ref.shape[1]
    for t in range(s // SEL_TK):
        vst_ref[0, 0, t] = vs_ref[0, t * SEL_TK:(t + 1) * SEL_TK, :].T.astype(BF16)
    for t in range(s // WIN_TK):
        vwt_ref[0, 0, t] = vw_ref[0, t * WIN_TK:(t + 1) * WIN_TK, :].T.astype(BF16)
    kcmp_ref[0, 0] = _compress(kc_ref, pek_ref, w1k_ref, w2k_ref).astype(BF16)
    vcmpt_ref[0, 0] = _compress(vc_ref, pev_ref, w1v_ref, w2v_ref).T.astype(BF16)


def _nsa_prep(z, rope_c, rope_a, rope_b, pe_k, w1_k, w2_k, pe_v, w1_v, w2_v):
    b, s, _ = z.shape
    first = D_MODEL // HEAD_DIM
    zspec = lambda k: pl.BlockSpec((1, s, HEAD_DIM), lambda i, g, k=k: (i, 0, first + k * N_KV + g))
    tab = pl.BlockSpec((1, s, HEAD_DIM), lambda i, g: (i, 0, 0))
    c2 = lambda shape: pl.BlockSpec(shape, lambda i, g: (0, 0))
    c3 = lambda shape: pl.BlockSpec(shape, lambda i, g: (0, 0, 0))
    out4 = lambda shape: pl.BlockSpec((1, 1) + shape, lambda i, g: (i, g, 0, 0))
    out5 = lambda shape: pl.BlockSpec((1, 1) + shape, lambda i, g: (i, g, 0, 0, 0))
    w1 = lambda w: w.reshape(CMP_LEN, HEAD_DIM, CMP_HIDDEN)
    return pl.pallas_call(
        _nsa_prep_kernel,
        grid=(b, N_KV),
        in_specs=[zspec(0), zspec(1), zspec(2), zspec(3), zspec(4), zspec(5), tab, tab, tab,
                  c2((CMP_LEN, HEAD_DIM)), c3((CMP_LEN, HEAD_DIM, CMP_HIDDEN)), c2((CMP_HIDDEN, HEAD_DIM)),
                  c2((CMP_LEN, HEAD_DIM)), c3((CMP_LEN, HEAD_DIM, CMP_HIDDEN)), c2((CMP_HIDDEN, HEAD_DIM))],
        out_specs=[out4((s, HEAD_DIM)), out5((s // SEL_TK, HEAD_DIM, SEL_TK)),
                   out4((s, HEAD_DIM)), out5((s // WIN_TK, HEAD_DIM, WIN_TK)),
                   out4((N_HALF, HEAD_DIM)), out4((HEAD_DIM, N_HALF))],
        out_shape=[jax.ShapeDtypeStruct((b, N_KV, s, HEAD_DIM), BF16),
                   jax.ShapeDtypeStruct((b, N_KV, s // SEL_TK, HEAD_DIM, SEL_TK), BF16),
                   jax.ShapeDtypeStruct((b, N_KV, s, HEAD_DIM), BF16),
                   jax.ShapeDtypeStruct((b, N_KV, s // WIN_TK, HEAD_DIM, WIN_TK), BF16),
                   jax.ShapeDtypeStruct((b, N_KV, N_HALF, HEAD_DIM), BF16),
                   jax.ShapeDtypeStruct((b, N_KV, HEAD_DIM, N_HALF), BF16)],
        compiler_params=_cparams(("parallel", "arbitrary"), 40),
        name="nsa_prep",
    )(z, z, z, z, z, z, rope_c, rope_a, rope_b,
      pe_k, w1(w1_k).astype(BF16), w2_k.astype(BF16), pe_v, w1(w1_v).astype(BF16), w2_v.astype(BF16))


Q_SCALE = SCALE * 1.4426950408889634


def _scores_t(k, q):
    return lax.dot_general(k, q, (((1,), (1,)), ((), ())), preferred_element_type=F32)


def _online_attend(q_parts, k_ref, vt_ref, tile, n_tiles, bias_fn):
    def body(kt, carry):
        k = k_ref[0, 0, pl.ds(pl.multiple_of(kt * tile, tile), tile), :]
        vt = vt_ref[0, 0, kt]
        bias = bias_fn(kt)
        out = []
        for q, (m, l, acc) in zip(q_parts, carry):
            s = _scores_t(k, q) + jnp.concatenate([bias] * (q.shape[0] // bias.shape[1]), axis=1)
            m_new = jnp.maximum(m, jnp.max(s, axis=0, keepdims=True))
            alpha = jnp.exp2(m - m_new)
            p = jnp.exp2(s - m_new)
            l = alpha * l + jnp.sum(p, axis=0, keepdims=True)
            acc = alpha * acc + jnp.dot(vt, p.astype(BF16), preferred_element_type=F32)
            out.append((m_new, l, acc))
        return tuple(out)

    init = tuple((jnp.full((1, q.shape[0]), NEG, F32), jnp.zeros((1, q.shape[0]), F32),
                  jnp.zeros((HEAD_DIM, q.shape[0]), F32)) for q in q_parts)
    res = lax.fori_loop(0, n_tiles, body, init)
    return jnp.concatenate([acc / l for (_, l, acc) in res], axis=1)


def _nsa_attn_kernel(q_ref, zg_ref, rc_ref, ra_ref, rb_ref, ksr_ref, vst_ref, kwr_ref, vwt_ref,
                     kcmp_ref, vcmpt_ref, ovt_ref, exp_ref, o_ref, bias_scr, gate_scr, *, tq, seq):
    grp = pl.program_id(1)
    t0 = pl.program_id(2) * tq
    rows = HPG * tq
    heads = lambda x: jnp.concatenate([x] * HPG, axis=1)

    c, a, b = rc_ref[0], ra_ref[0], rb_ref[0]
    q_plain, q_rot = [], []
    for h in range(HPG):
        qh = q_ref[0, :, h * HEAD_DIM:(h + 1) * HEAD_DIM] * Q_SCALE
        q_plain.append(qh.astype(BF16))
        q_rot.append(_rope(qh, c, a, b).astype(BF16))
    q_plain = jnp.concatenate(q_plain, axis=0)
    q_rot = jnp.concatenate(q_rot, axis=0)

    tpos = t0 + lax.broadcasted_iota(jnp.int32, (1, tq), 1)

    s_c = _scores_t(kcmp_ref[0, 0], q_plain)
    n_idx = lax.broadcasted_iota(jnp.int32, (N_HALF, rows), 0)
    cmask = n_idx * CMP_STRIDE + (CMP_LEN - 1) <= heads(tpos)
    s_c = jnp.where(cmask, s_c, NEG)
    e = jnp.exp2(s_c - jnp.max(s_c, axis=0, keepdims=True))
    p_c = jnp.where(cmask, e / jnp.sum(e, axis=0, keepdims=True), 0.0)
    o_c = jnp.dot(vcmpt_ref[0, 0], p_c.astype(BF16), preferred_element_type=F32)

    span = WINDOW + tq
    k0 = pl.multiple_of(jnp.maximum(t0 - WINDOW, 0), WIN_TK)
    kp = k0 + lax.broadcasted_iota(jnp.int32, (span, tq), 0)
    w_bias = jnp.where((kp <= tpos) & (kp > tpos - WINDOW), 0.0, NEG)
    s_w = _scores_t(kwr_ref[0, 0, pl.ds(k0, span), :], q_rot) + heads(w_bias)
    p_w = jnp.exp2(s_w - jnp.max(s_w, axis=0, keepdims=True))
    l_w = jnp.sum(p_w, axis=0, keepdims=True)
    p_w = p_w.astype(BF16)
    o_w = jnp.zeros((HEAD_DIM, rows), F32)
    for i in range(span // WIN_TK):
        o_w = o_w + jnp.dot(vwt_ref[0, 0, k0 // WIN_TK + i], p_w[i * WIN_TK:(i + 1) * WIN_TK, :],
                            preferred_element_type=F32)
    o_w = o_w / l_w

    p_sum = p_c[:, 0:tq]
    for h in range(1, HPG):
        p_sum = p_sum + p_c[:, h * tq:(h + 1) * tq]
    p_hi = p_sum.astype(BF16)
    p_lo = (p_sum - p_hi.astype(F32)).astype(BF16)
    ovt = ovt_ref[...]
    imp = (jnp.dot(ovt, p_hi, preferred_element_type=F32)
           + jnp.dot(ovt, p_lo, preferred_element_type=F32))
    n_blk = seq // SEL_BLOCK
    j_idx = lax.broadcasted_iota(jnp.int32, (n_blk, tq), 0)
    cur = tpos // SEL_BLOCK
    forced = (j_idx == 0) | (j_idx == cur) | (j_idx == cur - 1)
    imp = imp + jnp.where(forced, FORCE_BONUS, 0.0)
    imp = jnp.where(j_idx * SEL_BLOCK <= tpos, imp, NEG)
    rank = jnp.zeros((n_blk, tq), jnp.int32)
    for i in range(n_blk):
        vi = imp[i:i + 1, :]
        ahead = (vi > imp) | ((vi == imp) & (j_idx > i))
        rank = rank + ahead.astype(jnp.int32)
    sel = (rank < min(SEL_TOPK, n_blk)).astype(BF16)

    sel_keys = jnp.dot(exp_ref[...], sel, preferred_element_type=F32)
    kpos = lax.broadcasted_iota(jnp.int32, (seq, tq), 0)
    bias_scr[...] = jnp.where((sel_keys > 0.5) & (kpos <= tpos), 0.0, NEG)

    def sel_bias(kt):
        return bias_scr[pl.ds(pl.multiple_of(kt * SEL_TK, SEL_TK), SEL_TK), :]

    o_s = _online_attend((q_rot,), ksr_ref, vst_ref, SEL_TK,
                         (t0 + tq + SEL_TK - 1) // SEL_TK, sel_bias)

    gate_scr[...] = jax.nn.sigmoid(zg_ref[0]).T

    def gate(branch):
        return jnp.concatenate(
            [gate_scr[pl.ds(3 * (HPG * grp + h) + branch, 1), :] for h in range(HPG)], axis=1)

    o = gate(0) * o_c + gate(1) * o_s + gate(2) * o_w
    for h in range(HPG):
        o_ref[0, :, h * HEAD_DIM:(h + 1) * HEAD_DIM] = o[:, h * tq:(h + 1) * tq].T.astype(o_ref.dtype)


def _nsa_attn(z, rope_c, rope_a, rope_b, ksr, vst, kwr, vwt, kcmp, vcmpt, tq=128):
    b, s, _ = z.shape
    n_blk = s // SEL_BLOCK
    n = np.arange(N_HALF)[None, :]
    j = np.arange(n_blk)[:, None]
    ovt = ((n < N_HALF - 1) & (n * CMP_STRIDE < (j + 1) * SEL_BLOCK)
           & (n * CMP_STRIDE + CMP_LEN - 1 >= j * SEL_BLOCK))
    expand = (np.arange(s)[:, None] // SEL_BLOCK) == np.arange(n_blk)[None, :]
    gate_blk = NSA_MAIN // HEAD_DIM
    tab = pl.BlockSpec((1, tq, HEAD_DIM), lambda i, g, t: (i, t, 0))
    kv4 = lambda shape: pl.BlockSpec((1, 1) + shape, lambda i, g, t: (i, g, 0, 0))
    kv5 = lambda shape: pl.BlockSpec((1, 1) + shape, lambda i, g, t: (i, g, 0, 0, 0))
    return pl.pallas_call(
        functools.partial(_nsa_attn_kernel, tq=tq, seq=s),
        grid=(b, N_KV, s // tq),
        in_specs=[
            pl.BlockSpec((1, tq, HPG * HEAD_DIM), lambda i, g, t: (i, t, g)),
            pl.BlockSpec((1, tq, HEAD_DIM), lambda i, g, t: (i, t, gate_blk)),
            tab, tab, tab,
            kv4((s, HEAD_DIM)), kv5((s // SEL_TK, HEAD_DIM, SEL_TK)),
            kv4((s, HEAD_DIM)), kv5((s // WIN_TK, HEAD_DIM, WIN_TK)),
            kv4((N_HALF, HEAD_DIM)), kv4((HEAD_DIM, N_HALF)),
            pl.BlockSpec((n_blk, N_HALF), lambda i, g, t: (0, 0)),
            pl.BlockSpec((s, n_blk), lambda i, g, t: (0, 0)),
        ],
        out_specs=pl.BlockSpec((1, tq, HPG * HEAD_DIM), lambda i, g, t: (i, t, g)),
        out_shape=jax.ShapeDtypeStruct((b, s, D_MODEL), BF16),
        scratch_shapes=[pltpu.VMEM((s, tq), F32), pltpu.VMEM((HEAD_DIM, tq), F32)],
        compiler_params=_cparams(("parallel", "parallel", "arbitrary"), 40),
        name="nsa_attn",
    )(z, z, rope_c, rope_a, rope_b, ksr, vst, kwr, vwt, kcmp, vcmpt,
      jnp.asarray(ovt, BF16), jnp.asarray(expand, BF16))


def _rope_tables(positions):
    half = ROT_DIM // 2
    inv = ROPE_THETA ** (-jnp.arange(0, ROT_DIM, 2, dtype=F32) / ROT_DIM)
    ang = positions.astype(F32)[..., None] * inv
    cos, sin = jnp.cos(ang), jnp.sin(ang)
    zeros = lambda w: jnp.zeros(cos.shape[:-1] + (w,), F32)
    c = jnp.concatenate([cos, cos, jnp.ones(cos.shape[:-1] + (HEAD_DIM - ROT_DIM,), F32)], axis=-1)
    a = jnp.concatenate([-sin, zeros(HEAD_DIM - half)], axis=-1)
    b = jnp.concatenate([zeros(half), sin, zeros(HEAD_DIM - ROT_DIM)], axis=-1)
    return c, a, b


def kernel(x, positions, l0_ffn1_norm, l0_ffn1_w_gate, l0_ffn1_w_up, l0_ffn1_w_down, l0_mix_norm, l0_w_in, l0_conv_w, l0_conv_b, l0_conv_ln_g, l0_conv_ln_b, l0_gmlp_ln_g, l0_gmlp_ln_b, l0_gmlp_ws, l0_gmlp_bs, l0_w_out, l0_ffn2_norm, l0_ffn2_w_gate, l0_ffn2_w_up, l0_ffn2_w_down, l1_ffn1_norm, l1_ffn1_w_gate, l1_ffn1_w_up, l1_ffn1_w_down, l1_mix_norm, l1_w_in, l1_cmp_pe_k, l1_cmp_w1_k, l1_cmp_w2_k, l1_cmp_pe_v, l1_cmp_w1_v, l1_cmp_w2_v, l1_w_out, l1_ffn2_norm, l1_ffn2_w_gate, l1_ffn2_w_up, l1_ffn2_w_down, final_norm):
    b, s, d = x.shape
    m = b * s
    bf = lambda w: w.astype(BF16)
    x2 = x.reshape(m, d)

    x2 = _ffn(x2, l0_ffn1_norm, bf(l0_ffn1_w_gate), bf(l0_ffn1_w_up), bf(l0_ffn1_w_down))
    z = _norm_matmul(x2, l0_mix_norm, bf(l0_w_in)).reshape(b, s, -1)
    cat = _l0_mid(z, l0_conv_w, l0_conv_b, l0_conv_ln_g, l0_conv_ln_b,
                  l0_gmlp_ln_g, l0_gmlp_ln_b, l0_gmlp_ws, l0_gmlp_bs)
    x2 = _matmul_res(cat.reshape(m, -1), bf(l0_w_out), x2)
    x2 = _ffn(x2, l0_ffn2_norm, bf(l0_ffn2_w_gate), bf(l0_ffn2_w_up), bf(l0_ffn2_w_down))

    x2 = _ffn(x2, l1_ffn1_norm, bf(l1_ffn1_w_gate), bf(l1_ffn1_w_up), bf(l1_ffn1_w_down))
    w_in = jnp.pad(bf(l1_w_in), ((0, 0), (0, NSA_PAD - NSA_MAIN - NSA_GATES)))
    z = _norm_matmul(x2, l1_mix_norm, w_in).reshape(b, s, -1)
    rope_c, rope_a, rope_b = _rope_tables(positions)
    ksr, vst, kwr, vwt, kcmp, vcmpt = _nsa_prep(
        z, rope_c, rope_a, rope_b, l1_cmp_pe_k, l1_cmp_w1_k, l1_cmp_w2_k,
        l1_cmp_pe_v, l1_cmp_w1_v, l1_cmp_w2_v)
    o = _nsa_attn(z, rope_c, rope_a, rope_b, ksr, vst, kwr, vwt, kcmp, vcmpt)
    x2 = _matmul_res(o.reshape(m, -1), bf(l1_w_out), x2)
    x2 = _ffn(x2, l1_ffn2_norm, bf(l1_ffn2_w_gate), bf(l1_ffn2_w_up), bf(l1_ffn2_w_down),
              final_w=final_norm)
    return x2.reshape(b, s, d)
```

```python
import functools

import numpy as np
import jax
import jax.numpy as jnp
from jax import lax
from jax.experimental import pallas as pl
from jax.experimental.pallas import tpu as pltpu

F32 = jnp.float32
BF16 = jnp.bfloat16

D_MODEL = 2048
D_FF = 5632
EPS = 1e-6
CONV_CH = 1024
CONV_WIDTH = 31
GMLP_CH = 1024
GMLP_GROUPS = 8
GMLP_GDIM = 128
GMLP_CHUNK = 128
HEAD_DIM = 128
N_HEADS = 16
N_KV = 4
HPG = 4
KV_W = N_KV * HEAD_DIM
ROT_DIM = 32
ROPE_THETA = 500000.0
CMP_LEN = 32
CMP_STRIDE = 16
CMP_HIDDEN = 256
SEL_BLOCK = 64
SEL_TOPK = 16
WINDOW = 512
FORCE_BONUS = 1e3
NEG = -1e30
NSA_MAIN = D_MODEL + 6 * KV_W
NSA_GATES = 3 * N_HEADS
NSA_PAD = 5632
SCALE = HEAD_DIM ** -0.5

MIB = 1024 * 1024


def _cparams(semantics, vmem_mib):
    return pltpu.CompilerParams(dimension_semantics=semantics,
                                vmem_limit_bytes=vmem_mib * MIB)


def _rms_scale(x):
    return x * lax.rsqrt(jnp.mean(x * x, axis=-1, keepdims=True) + EPS)


def _ffn_kernel(x_ref, nw_ref, wg_ref, wu_ref, wd_ref, *rest, final):
    if final:
        fn_ref, o_ref, h_scr = rest
    else:
        o_ref, h_scr = rest
    f = pl.program_id(1)

    @pl.when(f == 0)
    def _():
        h_scr[...] = (_rms_scale(x_ref[...]) * nw_ref[...]).astype(BF16)
        o_ref[...] = jnp.zeros(o_ref.shape, F32)

    h = h_scr[...]
    g = jnp.dot(h, wg_ref[...].astype(BF16), preferred_element_type=F32)
    u = jnp.dot(h, wu_ref[...].astype(BF16), preferred_element_type=F32)
    a = (g * jax.nn.sigmoid(g) * u).astype(BF16)
    o_ref[...] += jnp.dot(a, wd_ref[...].astype(BF16), preferred_element_type=F32)

    @pl.when(f == pl.num_programs(1) - 1)
    def _():
        y = x_ref[...] + 0.5 * o_ref[...]
        if final:
            y = _rms_scale(y) * fn_ref[...]
        o_ref[...] = y


def _ffn(x2d, norm_w, wg, wu, wd, final_w=None, tm=1024, tf=256):
    m, d = x2d.shape
    dff = wg.shape[1]
    final = final_w is not None
    in_specs = [
        pl.BlockSpec((tm, d), lambda i, j: (i, 0), pipeline_mode=pl.Buffered(1)),
        pl.BlockSpec((1, d), lambda i, j: (0, 0)),
        pl.BlockSpec((d, tf), lambda i, j: (0, j)),
        pl.BlockSpec((d, tf), lambda i, j: (0, j)),
        pl.BlockSpec((tf, d), lambda i, j: (j, 0)),
    ]
    args = [x2d, norm_w.reshape(1, d), wg, wu, wd]
    if final:
        in_specs.append(pl.BlockSpec((1, d), lambda i, j: (0, 0)))
        args.append(final_w.reshape(1, d))
    return pl.pallas_call(
        functools.partial(_ffn_kernel, final=final),
        grid=(m // tm, dff // tf),
        in_specs=in_specs,
        out_specs=pl.BlockSpec((tm, d), lambda i, j: (i, 0), pipeline_mode=pl.Buffered(1)),
        out_shape=jax.ShapeDtypeStruct((m, d), F32),
        scratch_shapes=[pltpu.VMEM((tm, d), BF16)],
        compiler_params=_cparams(("parallel", "arbitrary"), 56),
        name="ffn",
    )(*args)


def _norm_matmul_kernel(x_ref, nw_ref, w_ref, o_ref, h_scr):
    @pl.when(pl.program_id(1) == 0)
    def _():
        h_scr[...] = (_rms_scale(x_ref[...]) * nw_ref[...]).astype(BF16)

    o_ref[...] = jnp.dot(h_scr[...], w_ref[...], preferred_element_type=F32)


def _norm_matmul(x2d, norm_w, w, tm=1024, tn=512):
    m, d = x2d.shape
    n = w.shape[1]
    return pl.pallas_call(
        _norm_matmul_kernel,
        grid=(m // tm, n // tn),
        in_specs=[
            pl.BlockSpec((tm, d), lambda i, j: (i, 0)),
            pl.BlockSpec((1, d), lambda i, j: (0, 0)),
            pl.BlockSpec((d, tn), lambda i, j: (0, j)),
        ],
        out_specs=pl.BlockSpec((tm, tn), lambda i, j: (i, j)),
        out_shape=jax.ShapeDtypeStruct((m, n), F32),
        scratch_shapes=[pltpu.VMEM((tm, d), BF16)],
        compiler_params=_cparams(("parallel", "arbitrary"), 40),
        name="norm_matmul",
    )(x2d, norm_w.reshape(1, d), w)


def _matmul_res_kernel(a_ref, w_ref, r_ref, o_ref):
    o_ref[...] = r_ref[...] + jnp.dot(a_ref[...], w_ref[...], preferred_element_type=F32)


def _matmul_res(a, w, res, tm=512, tn=1024):
    m, k = a.shape
    n = w.shape[1]
    return pl.pallas_call(
        _matmul_res_kernel,
        grid=(m // tm, n // tn),
        in_specs=[
            pl.BlockSpec((tm, k), lambda i, j: (i, 0)),
            pl.BlockSpec((k, tn), lambda i, j: (0, j)),
            pl.BlockSpec((tm, tn), lambda i, j: (i, j)),
        ],
        out_specs=pl.BlockSpec((tm, tn), lambda i, j: (i, j)),
        out_shape=jax.ShapeDtypeStruct((m, n), F32),
        compiler_params=_cparams(("parallel", "arbitrary"), 32),
        name="matmul_res",
    )(a, w, res)


CONV_HALO = 32
SUBLANES = 8


def _layernorm(x, g, b):
    mu = jnp.mean(x, axis=-1, keepdims=True)
    xc = x - mu
    var = jnp.mean(xc * xc, axis=-1, keepdims=True)
    return xc * lax.rsqrt(var + EPS) * g + b


def _l0_mid_kernel(av_ref, ag_ref, bu_ref, bv_ref, cw_ref, cb_ref, clg_ref, clb_ref,
                   glg_ref, glb_ref, ws_ref, bst_ref, o_ref, abuf, shifted, *, ts):
    s = pl.program_id(1)

    @pl.when(s == 0)
    def _():
        abuf[0:CONV_HALO, :] = jnp.zeros((CONV_HALO, CONV_CH), F32)

    abuf[CONV_HALO:CONV_HALO + ts, :] = av_ref[0] * jax.nn.sigmoid(ag_ref[0])
    span = CONV_HALO + ts - SUBLANES
    for r in range(1, SUBLANES):
        shifted[r - 1] = abuf[r:r + span, :]
    base = CONV_HALO - (CONV_WIDTH - 1)
    acc = jnp.zeros((ts, CONV_CH), F32)
    for j in range(CONV_WIDTH):
        q, r = divmod(base + j, SUBLANES)
        src = abuf if r == 0 else shifted.at[r - 1]
        rows = src[SUBLANES * q:SUBLANES * q + ts, :]
        acc = acc + rows * cw_ref[j:j + 1, :]
    acc = acc + cb_ref[...]
    a = _layernorm(acc, clg_ref[...], clb_ref[...])
    o_ref[0, :, 0:CONV_CH] = (a * jax.nn.sigmoid(a)).astype(o_ref.dtype)
    abuf[0:CONV_HALO, :] = abuf[ts:ts + CONV_HALO, :]

    row = lax.broadcasted_iota(jnp.int32, (GMLP_CHUNK, GMLP_CHUNK), 0)
    col = lax.broadcasted_iota(jnp.int32, (GMLP_CHUNK, GMLP_CHUNK), 1)
    tri = col <= row
    for g in range(GMLP_GROUPS):
        lo = g * GMLP_GDIM
        w_g = jnp.where(tri, ws_ref[g], 0.0).astype(BF16)
        bias_g = bst_ref[:, g:g + 1]
        ln_g = glg_ref[:, lo:lo + GMLP_GDIM]
        ln_b = glb_ref[:, lo:lo + GMLP_GDIM]
        for c in range(ts // GMLP_CHUNK):
            r0 = c * GMLP_CHUNK
            u = jax.nn.gelu(bu_ref[0, r0:r0 + GMLP_CHUNK, lo:lo + GMLP_GDIM])
            v = jax.nn.gelu(bv_ref[0, r0:r0 + GMLP_CHUNK, lo:lo + GMLP_GDIM])
            v = _layernorm(v, ln_g, ln_b)
            sv = jnp.dot(w_g, v.astype(BF16), preferred_element_type=F32) + bias_g
            o_ref[0, r0:r0 + GMLP_CHUNK, CONV_CH + lo:CONV_CH + lo + GMLP_GDIM] = (
                (u * sv).astype(o_ref.dtype))


def _l0_mid(z, conv_w, conv_b, conv_ln_g, conv_ln_b, gmlp_ln_g, gmlp_ln_b, gmlp_ws, gmlp_bs, ts=256):
    b, s, _ = z.shape
    vec = lambda a: a.reshape(1, -1)
    const2 = lambda shape: pl.BlockSpec(shape, lambda i, j: (0, 0))
    zspec = lambda k: pl.BlockSpec((1, ts, CONV_CH), lambda i, j, k=k: (i, j, k))
    return pl.pallas_call(
        functools.partial(_l0_mid_kernel, ts=ts),
        grid=(b, s // ts),
        in_specs=[
            zspec(0), zspec(1), zspec(2), zspec(3),
            const2((CONV_WIDTH, CONV_CH)),
            const2((1, CONV_CH)), const2((1, CONV_CH)), const2((1, CONV_CH)),
            const2((1, GMLP_CH)), const2((1, GMLP_CH)),
            pl.BlockSpec((GMLP_GROUPS, GMLP_CHUNK, GMLP_CHUNK), lambda i, j: (0, 0, 0)),
            const2((GMLP_CHUNK, GMLP_GROUPS)),
        ],
        out_specs=pl.BlockSpec((1, ts, CONV_CH + GMLP_CH), lambda i, j: (i, j, 0)),
        out_shape=jax.ShapeDtypeStruct((b, s, CONV_CH + GMLP_CH), BF16),
        scratch_shapes=[pltpu.VMEM((CONV_HALO + ts, CONV_CH), F32),
                        pltpu.VMEM((SUBLANES - 1, CONV_HALO + ts - SUBLANES, CONV_CH), F32)],
        compiler_params=_cparams(("parallel", "arbitrary"), 32),
        name="l0_mid",
    )(z, z, z, z, conv_w.reshape(CONV_WIDTH, CONV_CH), vec(conv_b), vec(conv_ln_g), vec(conv_ln_b),
      vec(gmlp_ln_g), vec(gmlp_ln_b), gmlp_ws, gmlp_bs.T)


SEL_TK = 512
WIN_TK = 128
N_HALF = 128


def _rope(x, c, a, b):
    return x * c + pltpu.roll(x, HEAD_DIM - ROT_DIM // 2, 1) * a + pltpu.roll(x, ROT_DIM // 2, 1) * b


def _compress(k_ref, pe_ref, w1_ref, w2_ref):
    half = CMP_LEN // 2
    p = jnp.zeros((N_HALF, CMP_HIDDEN), F32)
    q = jnp.zeros((N_HALF, CMP_HIDDEN), F32)
    for l in range(half):
        rows = k_ref[0, pl.ds(l, N_HALF, stride=CMP_STRIDE), :]
        p = p + jnp.dot((rows + pe_ref[l:l + 1, :]).astype(BF16), w1_ref[l],
                        preferred_element_type=F32)
        q = q + jnp.dot((rows + pe_ref[half + l:half + l + 1, :]).astype(BF16), w1_ref[half + l],
                        preferred_element_type=F32)
    hid = p + pltpu.roll(q, N_HALF - 1, 0)
    return jnp.dot(jax.nn.gelu(hid).astype(BF16), w2_ref[...], preferred_element_type=F32)


def _nsa_prep_kernel(kc_ref, vc_ref, ks_ref, vs_ref, kw_ref, vw_ref, rc_ref, ra_ref, rb_ref,
                     pek_ref, w1k_ref, w2k_ref, pev_ref, w1v_ref, w2v_ref,
                     ksr_ref, vst_ref, kwr_ref, vwt_ref, kcmp_ref, vcmpt_ref):
    c, a, b = rc_ref[0], ra_ref[0], rb_ref[0]
    ksr_ref[0, 0] = _rope(ks_ref[0], c, a, b).astype(BF16)
    kwr_ref[0, 0] = _rope(kw_ref[0], c, a, b).astype(BF16)
    s = vs_ref.shape[1]
    for t in range(s // SEL_TK):
        vst_ref[0, 0, t] = vs_ref[0, t * SEL_TK:(t + 1) * SEL_TK, :].T.astype(BF16)
    for t in range(s // WIN_TK):
        vwt_ref[0, 0, t] = vw_ref[0, t * WIN_TK:(t + 1) * WIN_TK, :].T.astype(BF16)
    kcmp_ref[0, 0] = _compress(kc_ref, pek_ref, w1k_ref, w2k_ref).astype(BF16)
    vcmpt_ref[0, 0] = _compress(vc_ref, pev_ref, w1v_ref, w2v_ref).T.astype(BF16)


def _nsa_prep(z, rope_c, rope_a, rope_b, pe_k, w1_k, w2_k, pe_v, w1_v, w2_v):
    b, s, _ = z.shape
    first = D_MODEL // HEAD_DIM
    zspec = lambda k: pl.BlockSpec((1, s, HEAD_DIM), lambda i, g, k=k: (i, 0, first + k * N_KV + g))
    tab = pl.BlockSpec((1, s, HEAD_DIM), lambda i, g: (i, 0, 0))
    c2 = lambda shape: pl.BlockSpec(shape, lambda i, g: (0, 0))
    c3 = lambda shape: pl.BlockSpec(shape, lambda i, g: (0, 0, 0))
    out4 = lambda shape: pl.BlockSpec((1, 1) + shape, lambda i, g: (i, g, 0, 0))
    out5 = lambda shape: pl.BlockSpec((1, 1) + shape, lambda i, g: (i, g, 0, 0, 0))
    w1 = lambda w: w.reshape(CMP_LEN, HEAD_DIM, CMP_HIDDEN)
    return pl.pallas_call(
        _nsa_prep_kernel,
        grid=(b, N_KV),
        in_specs=[zspec(0), zspec(1), zspec(2), zspec(3), zspec(4), zspec(5), tab, tab, tab,
                  c2((CMP_LEN, HEAD_DIM)), c3((CMP_LEN, HEAD_DIM, CMP_HIDDEN)), c2((CMP_HIDDEN, HEAD_DIM)),
                  c2((CMP_LEN, HEAD_DIM)), c3((CMP_LEN, HEAD_DIM, CMP_HIDDEN)), c2((CMP_HIDDEN, HEAD_DIM))],
        out_specs=[out4((s, HEAD_DIM)), out5((s // SEL_TK, HEAD_DIM, SEL_TK)),
                   out4((s, HEAD_DIM)), out5((s // WIN_TK, HEAD_DIM, WIN_TK)),
                   out4((N_HALF, HEAD_DIM)), out4((HEAD_DIM, N_HALF))],
        out_shape=[jax.ShapeDtypeStruct((b, N_KV, s, HEAD_DIM), BF16),
                   jax.ShapeDtypeStruct((b, N_KV, s // SEL_TK, HEAD_DIM, SEL_TK), BF16),
                   jax.ShapeDtypeStruct((b, N_KV, s, HEAD_DIM), BF16),
                   jax.ShapeDtypeStruct((b, N_KV, s // WIN_TK, HEAD_DIM, WIN_TK), BF16),
                   jax.ShapeDtypeStruct((b, N_KV, N_HALF, HEAD_DIM), BF16),
                   jax.ShapeDtypeStruct((b, N_KV, HEAD_DIM, N_HALF), BF16)],
        compiler_params=_cparams(("parallel", "arbitrary"), 40),
        name="nsa_prep",
    )(z, z, z, z, z, z, rope_c, rope_a, rope_b,
      pe_k, w1(w1_k).astype(BF16), w2_k.astype(BF16), pe_v, w1(w1_v).astype(BF16), w2_v.astype(BF16))


Q_SCALE = SCALE * 1.4426950408889634


def _scores_t(k, q):
    return lax.dot_general(k, q, (((1,), (1,)), ((), ())), preferred_element_type=F32)


def _online_attend(q, k_ref, vt_ref, bias_ref, tile, n_tiles):
    rows = q.shape[0]
    m = jnp.full((1, rows), NEG, F32)
    l = jnp.zeros((1, rows), F32)
    acc = jnp.zeros((HEAD_DIM, rows), F32)
    for kt in range(n_tiles):
        bias = bias_ref[kt * tile:(kt + 1) * tile, :]
        s = (_scores_t(k_ref[0, 0, kt * tile:(kt + 1) * tile, :], q)
             + jnp.concatenate([bias] * (rows // bias.shape[1]), axis=1))
        m_new = jnp.maximum(m, jnp.max(s, axis=0, keepdims=True))
        alpha = jnp.exp2(m - m_new)
        p = jnp.exp2(s - m_new)
        l = alpha * l + jnp.sum(p, axis=0, keepdims=True)
        acc = alpha * acc + jnp.dot(vt_ref[0, 0, kt], p.astype(BF16), preferred_element_type=F32)
        m = m_new
    return acc / l


def _nsa_attn_kernel(q_ref, zg_ref, rc_ref, ra_ref, rb_ref, ksr_ref, vst_ref, kwr_ref, vwt_ref,
                     kcmp_ref, vcmpt_ref, ovt_ref, exp_ref, o_ref, bias_scr, gate_scr, osel_scr, *, tq, seq):
    grp = pl.program_id(1)
    t0 = pl.program_id(2) * tq
    rows = HPG * tq
    heads = lambda x: jnp.concatenate([x] * HPG, axis=1)

    c, a, b = rc_ref[0], ra_ref[0], rb_ref[0]
    q_plain, q_rot = [], []
    for h in range(HPG):
        qh = q_ref[0, :, h * HEAD_DIM:(h + 1) * HEAD_DIM] * Q_SCALE
        q_plain.append(qh.astype(BF16))
        q_rot.append(_rope(qh, c, a, b).astype(BF16))
    q_plain = jnp.concatenate(q_plain, axis=0)
    q_rot = jnp.concatenate(q_rot, axis=0)

    tpos = t0 + lax.broadcasted_iota(jnp.int32, (1, tq), 1)

    s_c = _scores_t(kcmp_ref[0, 0], q_plain)
    n_idx = lax.broadcasted_iota(jnp.int32, (N_HALF, rows), 0)
    cmask = n_idx * CMP_STRIDE + (CMP_LEN - 1) <= heads(tpos)
    s_c = jnp.where(cmask, s_c, NEG)
    e = jnp.exp2(s_c - jnp.max(s_c, axis=0, keepdims=True))
    p_c = jnp.where(cmask, e / jnp.sum(e, axis=0, keepdims=True), 0.0)
    o_c = jnp.dot(vcmpt_ref[0, 0], p_c.astype(BF16), preferred_element_type=F32)

    span = WINDOW + tq
    k0 = pl.multiple_of(jnp.maximum(t0 - WINDOW, 0), WIN_TK)
    kp = k0 + lax.broadcasted_iota(jnp.int32, (span, tq), 0)
    w_bias = jnp.where((kp <= tpos) & (kp > tpos - WINDOW), 0.0, NEG)
    s_w = _scores_t(kwr_ref[0, 0, pl.ds(k0, span), :], q_rot) + heads(w_bias)
    p_w = jnp.exp2(s_w - jnp.max(s_w, axis=0, keepdims=True))
    l_w = jnp.sum(p_w, axis=0, keepdims=True)
    p_w = p_w.astype(BF16)
    o_w = jnp.zeros((HEAD_DIM, rows), F32)
    for i in range(span // WIN_TK):
        o_w = o_w + jnp.dot(vwt_ref[0, 0, k0 // WIN_TK + i], p_w[i * WIN_TK:(i + 1) * WIN_TK, :],
                            preferred_element_type=F32)
    o_w = o_w / l_w

    p_sum = p_c[:, 0:tq]
    for h in range(1, HPG):
        p_sum = p_sum + p_c[:, h * tq:(h + 1) * tq]
    p_hi = p_sum.astype(BF16)
    p_lo = (p_sum - p_hi.astype(F32)).astype(BF16)
    ovt = ovt_ref[...]
    imp = (jnp.dot(ovt, p_hi, preferred_element_type=F32)
           + jnp.dot(ovt, p_lo, preferred_element_type=F32))
    n_blk = seq // SEL_BLOCK
    j_idx = lax.broadcasted_iota(jnp.int32, (n_blk, tq), 0)
    cur = tpos // SEL_BLOCK
    forced = (j_idx == 0) | (j_idx == cur) | (j_idx == cur - 1)
    imp = imp + jnp.where(forced, FORCE_BONUS, 0.0)
    imp = jnp.where(j_idx * SEL_BLOCK <= tpos, imp, NEG)
    rank = jnp.zeros((n_blk, tq), jnp.int32)
    for i in range(n_blk):
        vi = imp[i:i + 1, :]
        ahead = (vi > imp) | ((vi == imp) & (j_idx > i))
        rank = rank + ahead.astype(jnp.int32)
    sel = (rank < min(SEL_TOPK, n_blk)).astype(BF16)

    sel_keys = jnp.dot(exp_ref[...], sel, preferred_element_type=F32)
    kpos = lax.broadcasted_iota(jnp.int32, (seq, tq), 0)
    bias_scr[...] = jnp.where((sel_keys > 0.5) & (kpos <= tpos), 0.0, NEG)

    n_tiles = (t0 + tq + SEL_TK - 1) // SEL_TK
    for n in range(1, seq // SEL_TK + 1):
        @pl.when(n_tiles == n)
        def _(n=n):
            osel_scr[...] = _online_attend(q_rot, ksr_ref, vst_ref, bias_scr, SEL_TK, n)
    o_s = osel_scr[...]

    gate_scr[...] = jax.nn.sigmoid(zg_ref[0]).T

    def gate(branch):
        return jnp.concatenate(
            [gate_scr[pl.ds(3 * (HPG * grp + h) + branch, 1), :] for h in range(HPG)], axis=1)

    o = gate(0) * o_c + gate(1) * o_s + gate(2) * o_w
    for h in range(HPG):
        o_ref[0, :, h * HEAD_DIM:(h + 1) * HEAD_DIM] = o[:, h * tq:(h + 1) * tq].T.astype(o_ref.dtype)


def _nsa_attn(z, rope_c, rope_a, rope_b, ksr, vst, kwr, vwt, kcmp, vcmpt, tq=128):
    b, s, _ = z.shape
    n_blk = s // SEL_BLOCK
    n = np.arange(N_HALF)[None, :]
    j = np.arange(n_blk)[:, None]
    ovt = ((n < N_HALF - 1) & (n * CMP_STRIDE < (j + 1) * SEL_BLOCK)
           & (n * CMP_STRIDE + CMP_LEN - 1 >= j * SEL_BLOCK))
    expand = (np.arange(s)[:, None] // SEL_BLOCK) == np.arange(n_blk)[None, :]
    gate_blk = NSA_MAIN // HEAD_DIM
    tab = pl.BlockSpec((1, tq, HEAD_DIM), lambda i, g, t: (i, t, 0))
    kv4 = lambda shape: pl.BlockSpec((1, 1) + shape, lambda i, g, t: (i, g, 0, 0))
    kv5 = lambda shape: pl.BlockSpec((1, 1) + shape, lambda i, g, t: (i, g, 0, 0, 0))
    return pl.pallas_call(
        functools.partial(_nsa_attn_kernel, tq=tq, seq=s),
        grid=(b, N_KV, s // tq),
        in_specs=[
            pl.BlockSpec((1, tq, HPG * HEAD_DIM), lambda i, g, t: (i, t, g)),
            pl.BlockSpec((1, tq, HEAD_DIM), lambda i, g, t: (i, t, gate_blk)),
            tab, tab, tab,
            kv4((s, HEAD_DIM)), kv5((s // SEL_TK, HEAD_DIM, SEL_TK)),
            kv4((s, HEAD_DIM)), kv5((s // WIN_TK, HEAD_DIM, WIN_TK)),
            kv4((N_HALF, HEAD_DIM)), kv4((HEAD_DIM, N_HALF)),
            pl.BlockSpec((n_blk, N_HALF), lambda i, g, t: (0, 0)),
            pl.BlockSpec((s, n_blk), lambda i, g, t: (0, 0)),
        ],
        out_specs=pl.BlockSpec((1, tq, HPG * HEAD_DIM), lambda i, g, t: (i, t, g)),
        out_shape=jax.ShapeDtypeStruct((b, s, D_MODEL), BF16),
        scratch_shapes=[pltpu.VMEM((s, tq), F32), pltpu.VMEM((HEAD_DIM, tq), F32),
                        pltpu.VMEM((HEAD_DIM, HPG * tq), F32)],
        compiler_params=_cparams(("parallel", "parallel", "arbitrary"), 40),
        name="nsa_attn",
    )(z, z, rope_c, rope_a, rope_b, ksr, vst, kwr, vwt, kcmp, vcmpt,
      jnp.asarray(ovt, BF16), jnp.asarray(expand, BF16))


def _rope_tables(positions):
    half = ROT_DIM // 2
    inv = ROPE_THETA ** (-jnp.arange(0, ROT_DIM, 2, dtype=F32) / ROT_DIM)
    ang = positions.astype(F32)[..., None] * inv
    cos, sin = jnp.cos(ang), jnp.sin(ang)
    zeros = lambda w: jnp.zeros(cos.shape[:-1] + (w,), F32)
    c = jnp.concatenate([cos, cos, jnp.ones(cos.shape[:-1] + (HEAD_DIM - ROT_DIM,), F32)], axis=-1)
    a = jnp.concatenate([-sin, zeros(HEAD_DIM - half)], axis=-1)
    b = jnp.concatenate([zeros(half), sin, zeros(HEAD_DIM - ROT_DIM)], axis=-1)
    return c, a, b


def kernel(x, positions, l0_ffn1_norm, l0_ffn1_w_gate, l0_ffn1_w_up, l0_ffn1_w_down, l0_mix_norm, l0_w_in, l0_conv_w, l0_conv_b, l0_conv_ln_g, l0_conv_ln_b, l0_gmlp_ln_g, l0_gmlp_ln_b, l0_gmlp_ws, l0_gmlp_bs, l0_w_out, l0_ffn2_norm, l0_ffn2_w_gate, l0_ffn2_w_up, l0_ffn2_w_down, l1_ffn1_norm, l1_ffn1_w_gate, l1_ffn1_w_up, l1_ffn1_w_down, l1_mix_norm, l1_w_in, l1_cmp_pe_k, l1_cmp_w1_k, l1_cmp_w2_k, l1_cmp_pe_v, l1_cmp_w1_v, l1_cmp_w2_v, l1_w_out, l1_ffn2_norm, l1_ffn2_w_gate, l1_ffn2_w_up, l1_ffn2_w_down, final_norm):
    b, s, d = x.shape
    m = b * s
    bf = lambda w: w.astype(BF16)
    x2 = x.reshape(m, d)

    x2 = _ffn(x2, l0_ffn1_norm, l0_ffn1_w_gate, l0_ffn1_w_up, l0_ffn1_w_down)
    z = _norm_matmul(x2, l0_mix_norm, bf(l0_w_in)).reshape(b, s, -1)
    cat = _l0_mid(z, l0_conv_w, l0_conv_b, l0_conv_ln_g, l0_conv_ln_b,
                  l0_gmlp_ln_g, l0_gmlp_ln_b, l0_gmlp_ws, l0_gmlp_bs)
    x2 = _matmul_res(cat.reshape(m, -1), bf(l0_w_out), x2)
    x2 = _ffn(x2, l0_ffn2_norm, l0_ffn2_w_gate, l0_ffn2_w_up, l0_ffn2_w_down)

    x2 = _ffn(x2, l1_ffn1_norm, l1_ffn1_w_gate, l1_ffn1_w_up, l1_ffn1_w_down)
    w_in = jnp.pad(bf(l1_w_in), ((0, 0), (0, NSA_PAD - NSA_MAIN - NSA_GATES)))
    z = _norm_matmul(x2, l1_mix_norm, w_in).reshape(b, s, -1)
    rope_c, rope_a, rope_b = _rope_tables(positions)
    ksr, vst, kwr, vwt, kcmp, vcmpt = _nsa_prep(
        z, rope_c, rope_a, rope_b, l1_cmp_pe_k, l1_cmp_w1_k, l1_cmp_w2_k,
        l1_cmp_pe_v, l1_cmp_w1_v, l1_cmp_w2_v)
    o = _nsa_attn(z, rope_c, rope_a, rope_b, ksr, vst, kwr, vwt, kcmp, vcmpt)
    x2 = _matmul_res(o.reshape(m, -1), bf(l1_w_out), x2)
    x2 = _ffn(x2, l1_ffn2_norm, l1_ffn2_w_gate, l1_ffn2_w_up, l1_ffn2_w_down,
              final_w=final_norm)
    return x2.reshape(b, s, d)
```

```python
import functools

import numpy as np
import jax
import jax.numpy as jnp
from jax import lax
from jax.experimental import pallas as pl
from jax.experimental.pallas import tpu as pltpu

F32 = jnp.float32
BF16 = jnp.bfloat16

D_MODEL = 2048
D_FF = 5632
EPS = 1e-6
CONV_CH = 1024
CONV_WIDTH = 31
GMLP_CH = 1024
GMLP_GROUPS = 8
GMLP_GDIM = 128
GMLP_CHUNK = 128
HEAD_DIM = 128
N_HEADS = 16
N_KV = 4
HPG = 4
KV_W = N_KV * HEAD_DIM
ROT_DIM = 32
ROPE_THETA = 500000.0
CMP_LEN = 32
CMP_STRIDE = 16
CMP_HIDDEN = 256
SEL_BLOCK = 64
SEL_TOPK = 16
WINDOW = 512
FORCE_BONUS = 1e3
NEG = -1e30
NSA_MAIN = D_MODEL + 6 * KV_W
NSA_GATES = 3 * N_HEADS
NSA_PAD = 5632
SCALE = HEAD_DIM ** -0.5

MIB = 1024 * 1024


def _cparams(semantics, vmem_mib):
    return pltpu.CompilerParams(dimension_semantics=semantics,
                                vmem_limit_bytes=vmem_mib * MIB)


def _rms_scale(x):
    return x * lax.rsqrt(jnp.mean(x * x, axis=-1, keepdims=True) + EPS)


def _ffn_kernel(x_ref, nw_ref, wg_ref, wu_ref, wd_ref, *rest, final):
    if final:
        fn_ref, o_ref, h_scr = rest
    else:
        o_ref, h_scr = rest
    f = pl.program_id(1)

    @pl.when(f == 0)
    def _():
        h_scr[...] = (_rms_scale(x_ref[...]) * nw_ref[...]).astype(BF16)
        o_ref[...] = jnp.zeros(o_ref.shape, F32)

    h = h_scr[...]
    g = jnp.dot(h, wg_ref[...].astype(BF16), preferred_element_type=F32)
    u = jnp.dot(h, wu_ref[...].astype(BF16), preferred_element_type=F32)
    a = (g * jax.nn.sigmoid(g) * u).astype(BF16)
    o_ref[...] += jnp.dot(a, wd_ref[...].astype(BF16), preferred_element_type=F32)

    @pl.when(f == pl.num_programs(1) - 1)
    def _():
        y = x_ref[...] + 0.5 * o_ref[...]
        if final:
            y = _rms_scale(y) * fn_ref[...]
        o_ref[...] = y


def _ffn(x2d, norm_w, wg, wu, wd, final_w=None, tm=1024, tf=256):
    m, d = x2d.shape
    dff = wg.shape[1]
    final = final_w is not None
    in_specs = [
        pl.BlockSpec((tm, d), lambda i, j: (i, 0), pipeline_mode=pl.Buffered(1)),
        pl.BlockSpec((1, d), lambda i, j: (0, 0)),
        pl.BlockSpec((d, tf), lambda i, j: (0, j)),
        pl.BlockSpec((d, tf), lambda i, j: (0, j)),
        pl.BlockSpec((tf, d), lambda i, j: (j, 0)),
    ]
    args = [x2d, norm_w.reshape(1, d), wg, wu, wd]
    if final:
        in_specs.append(pl.BlockSpec((1, d), lambda i, j: (0, 0)))
        args.append(final_w.reshape(1, d))
    return pl.pallas_call(
        functools.partial(_ffn_kernel, final=final),
        grid=(m // tm, dff // tf),
        in_specs=in_specs,
        out_specs=pl.BlockSpec((tm, d), lambda i, j: (i, 0), pipeline_mode=pl.Buffered(1)),
        out_shape=jax.ShapeDtypeStruct((m, d), F32),
        scratch_shapes=[pltpu.VMEM((tm, d), BF16)],
        compiler_params=_cparams(("parallel", "arbitrary"), 56),
        name="ffn",
    )(*args)


def _norm_matmul_kernel(x_ref, nw_ref, w_ref, o_ref, h_scr):
    @pl.when(pl.program_id(1) == 0)
    def _():
        h_scr[...] = (_rms_scale(x_ref[...]) * nw_ref[...]).astype(BF16)

    o_ref[...] = jnp.dot(h_scr[...], w_ref[...], preferred_element_type=F32)


def _norm_matmul(x2d, norm_w, w, tm=1024, tn=512):
    m, d = x2d.shape
    n = w.shape[1]
    return pl.pallas_call(
        _norm_matmul_kernel,
        grid=(m // tm, n // tn),
        in_specs=[
            pl.BlockSpec((tm, d), lambda i, j: (i, 0)),
            pl.BlockSpec((1, d), lambda i, j: (0, 0)),
            pl.BlockSpec((d, tn), lambda i, j: (0, j)),
        ],
        out_specs=pl.BlockSpec((tm, tn), lambda i, j: (i, j)),
        out_shape=jax.ShapeDtypeStruct((m, n), F32),
        scratch_shapes=[pltpu.VMEM((tm, d), BF16)],
        compiler_params=_cparams(("parallel", "arbitrary"), 40),
        name="norm_matmul",
    )(x2d, norm_w.reshape(1, d), w)


def _matmul_res_kernel(a_ref, w_ref, r_ref, o_ref):
    o_ref[...] = r_ref[...] + jnp.dot(a_ref[...], w_ref[...], preferred_element_type=F32)


def _matmul_res(a, w, res, tm=512, tn=1024):
    m, k = a.shape
    n = w.shape[1]
    return pl.pallas_call(
        _matmul_res_kernel,
        grid=(m // tm, n // tn),
        in_specs=[
            pl.BlockSpec((tm, k), lambda i, j: (i, 0)),
            pl.BlockSpec((k, tn), lambda i, j: (0, j)),
            pl.BlockSpec((tm, tn), lambda i, j: (i, j)),
        ],
        out_specs=pl.BlockSpec((tm, tn), lambda i, j: (i, j)),
        out_shape=jax.ShapeDtypeStruct((m, n), F32),
        compiler_params=_cparams(("parallel", "arbitrary"), 32),
        name="matmul_res",
    )(a, w, res)


CONV_HALO = 32
SUBLANES = 8


def _layernorm(x, g, b):
    mu = jnp.mean(x, axis=-1, keepdims=True)
    xc = x - mu
    var = jnp.mean(xc * xc, axis=-1, keepdims=True)
    return xc * lax.rsqrt(var + EPS) * g + b


def _l0_mid_kernel(av_ref, ag_ref, bu_ref, bv_ref, cw_ref, cb_ref, clg_ref, clb_ref,
                   glg_ref, glb_ref, ws_ref, bst_ref, o_ref, abuf, shifted, *, ts):
    s = pl.program_id(1)

    @pl.when(s == 0)
    def _():
        abuf[0:CONV_HALO, :] = jnp.zeros((CONV_HALO, CONV_CH), F32)

    abuf[CONV_HALO:CONV_HALO + ts, :] = av_ref[0] * jax.nn.sigmoid(ag_ref[0])
    span = CONV_HALO + ts - SUBLANES
    for r in range(1, SUBLANES):
        shifted[r - 1] = abuf[r:r + span, :]
    base = CONV_HALO - (CONV_WIDTH - 1)
    acc = jnp.zeros((ts, CONV_CH), F32)
    for j in range(CONV_WIDTH):
        q, r = divmod(base + j, SUBLANES)
        src = abuf if r == 0 else shifted.at[r - 1]
        rows = src[SUBLANES * q:SUBLANES * q + ts, :]
        acc = acc + rows * cw_ref[j:j + 1, :]
    acc = acc + cb_ref[...]
    a = _layernorm(acc, clg_ref[...], clb_ref[...])
    o_ref[0, :, 0:CONV_CH] = (a * jax.nn.sigmoid(a)).astype(o_ref.dtype)
    abuf[0:CONV_HALO, :] = abuf[ts:ts + CONV_HALO, :]

    row = lax.broadcasted_iota(jnp.int32, (GMLP_CHUNK, GMLP_CHUNK), 0)
    col = lax.broadcasted_iota(jnp.int32, (GMLP_CHUNK, GMLP_CHUNK), 1)
    tri = col <= row
    for g in range(GMLP_GROUPS):
        lo = g * GMLP_GDIM
        w_g = jnp.where(tri, ws_ref[g], 0.0).astype(BF16)
        bias_g = bst_ref[:, g:g + 1]
        ln_g = glg_ref[:, lo:lo + GMLP_GDIM]
        ln_b = glb_ref[:, lo:lo + GMLP_GDIM]
        for c in range(ts // GMLP_CHUNK):
            r0 = c * GMLP_CHUNK
            u = jax.nn.gelu(bu_ref[0, r0:r0 + GMLP_CHUNK, lo:lo + GMLP_GDIM])
            v = jax.nn.gelu(bv_ref[0, r0:r0 + GMLP_CHUNK, lo:lo + GMLP_GDIM])
            v = _layernorm(v, ln_g, ln_b)
            sv = jnp.dot(w_g, v.astype(BF16), preferred_element_type=F32) + bias_g
            o_ref[0, r0:r0 + GMLP_CHUNK, CONV_CH + lo:CONV_CH + lo + GMLP_GDIM] = (
                (u * sv).astype(o_ref.dtype))


def _l0_mid(z, conv_w, conv_b, conv_ln_g, conv_ln_b, gmlp_ln_g, gmlp_ln_b, gmlp_ws, gmlp_bs, ts=256):
    b, s, _ = z.shape
    vec = lambda a: a.reshape(1, -1)
    const2 = lambda shape: pl.BlockSpec(shape, lambda i, j: (0, 0))
    zspec = lambda k: pl.BlockSpec((1, ts, CONV_CH), lambda i, j, k=k: (i, j, k))
    return pl.pallas_call(
        functools.partial(_l0_mid_kernel, ts=ts),
        grid=(b, s // ts),
        in_specs=[
            zspec(0), zspec(1), zspec(2), zspec(3),
            const2((CONV_WIDTH, CONV_CH)),
            const2((1, CONV_CH)), const2((1, CONV_CH)), const2((1, CONV_CH)),
            const2((1, GMLP_CH)), const2((1, GMLP_CH)),
            pl.BlockSpec((GMLP_GROUPS, GMLP_CHUNK, GMLP_CHUNK), lambda i, j: (0, 0, 0)),
            const2((GMLP_CHUNK, GMLP_GROUPS)),
        ],
        out_specs=pl.BlockSpec((1, ts, CONV_CH + GMLP_CH), lambda i, j: (i, j, 0)),
        out_shape=jax.ShapeDtypeStruct((b, s, CONV_CH + GMLP_CH), BF16),
        scratch_shapes=[pltpu.VMEM((CONV_HALO + ts, CONV_CH), F32),
                        pltpu.VMEM((SUBLANES - 1, CONV_HALO + ts - SUBLANES, CONV_CH), F32)],
        compiler_params=_cparams(("parallel", "arbitrary"), 32),
        name="l0_mid",
    )(z, z, z, z, conv_w.reshape(CONV_WIDTH, CONV_CH), vec(conv_b), vec(conv_ln_g), vec(conv_ln_b),
      vec(gmlp_ln_g), vec(gmlp_ln_b), gmlp_ws, gmlp_bs.T)


SEL_TK = 512
WIN_TK = 128
N_HALF = 128


def _rope(x, c, a, b):
    return x * c + pltpu.roll(x, HEAD_DIM - ROT_DIM // 2, 1) * a + pltpu.roll(x, ROT_DIM // 2, 1) * b


def _compress(k_ref, pe_ref, w1_ref, w2_ref):
    half = CMP_LEN // 2
    p = jnp.zeros((N_HALF, CMP_HIDDEN), F32)
    q = jnp.zeros((N_HALF, CMP_HIDDEN), F32)
    for l in range(half):
        rows = k_ref[0, pl.ds(l, N_HALF, stride=CMP_STRIDE), :]
        p = p + jnp.dot((rows + pe_ref[l:l + 1, :]).astype(BF16), w1_ref[l],
                        preferred_element_type=F32)
        q = q + jnp.dot((rows + pe_ref[half + l:half + l + 1, :]).astype(BF16), w1_ref[half + l],
                        preferred_element_type=F32)
    hid = p + pltpu.roll(q, N_HALF - 1, 0)
    return jnp.dot(jax.nn.gelu(hid).astype(BF16), w2_ref[...], preferred_element_type=F32)


def _nsa_prep_kernel(kc_ref, vc_ref, ks_ref, vs_ref, kw_ref, vw_ref, rc_ref, ra_ref, rb_ref,
                     pek_ref, w1k_ref, w2k_ref, pev_ref, w1v_ref, w2v_ref,
                     ksr_ref, vst_ref, kwr_ref, vwt_ref, kcmp_ref, vcmpt_ref):
    c, a, b = rc_ref[0], ra_ref[0], rb_ref[0]
    ksr_ref[0, 0] = _rope(ks_ref[0], c, a, b).astype(BF16)
    kwr_ref[0, 0] = _rope(kw_ref[0], c, a, b).astype(BF16)
    s = vs_ref.shape[1]
    for t in range(s // SEL_TK):
        vst_ref[0, 0, t] = vs_ref[0, t * SEL_TK:(t + 1) * SEL_TK, :].T.astype(BF16)
    for t in range(s // WIN_TK):
        vwt_ref[0, 0, t] = vw_ref[0, t * WIN_TK:(t + 1) * WIN_TK, :].T.astype(BF16)
    kcmp_ref[0, 0] = _compress(kc_ref, pek_ref, w1k_ref, w2k_ref).astype(BF16)
    vcmpt_ref[0, 0] = _compress(vc_ref, pev_ref, w1v_ref, w2v_ref).T.astype(BF16)


def _nsa_prep(z, rope_c, rope_a, rope_b, pe_k, w1_k, w2_k, pe_v, w1_v, w2_v):
    b, s, _ = z.shape
    first = D_MODEL // HEAD_DIM
    zspec = lambda k: pl.BlockSpec((1, s, HEAD_DIM), lambda i, g, k=k: (i, 0, first + k * N_KV + g))
    tab = pl.BlockSpec((1, s, HEAD_DIM), lambda i, g: (i, 0, 0))
    c2 = lambda shape: pl.BlockSpec(shape, lambda i, g: (0, 0))
    c3 = lambda shape: pl.BlockSpec(shape, lambda i, g: (0, 0, 0))
    out4 = lambda shape: pl.BlockSpec((1, 1) + shape, lambda i, g: (i, g, 0, 0))
    out5 = lambda shape: pl.BlockSpec((1, 1) + shape, lambda i, g: (i, g, 0, 0, 0))
    w1 = lambda w: w.reshape(CMP_LEN, HEAD_DIM, CMP_HIDDEN)
    return pl.pallas_call(
        _nsa_prep_kernel,
        grid=(b, N_KV),
        in_specs=[zspec(0), zspec(1), zspec(2), zspec(3), zspec(4), zspec(5), tab, tab, tab,
                  c2((CMP_LEN, HEAD_DIM)), c3((CMP_LEN, HEAD_DIM, CMP_HIDDEN)), c2((CMP_HIDDEN, HEAD_DIM)),
                  c2((CMP_LEN, HEAD_DIM)), c3((CMP_LEN, HEAD_DIM, CMP_HIDDEN)), c2((CMP_HIDDEN, HEAD_DIM))],
        out_specs=[out4((s, HEAD_DIM)), out5((s // SEL_TK, HEAD_DIM, SEL_TK)),
                   out4((s, HEAD_DIM)), out5((s // WIN_TK, HEAD_DIM, WIN_TK)),
                   out4((N_HALF, HEAD_DIM)), out4((HEAD_DIM, N_HALF))],
        out_shape=[jax.ShapeDtypeStruct((b, N_KV, s, HEAD_DIM), BF16),
                   jax.ShapeDtypeStruct((b, N_KV, s // SEL_TK, HEAD_DIM, SEL_TK), BF16),
                   jax.ShapeDtypeStruct((b, N_KV, s, HEAD_DIM), BF16),
                   jax.ShapeDtypeStruct((b, N_KV, s // WIN_TK, HEAD_DIM, WIN_TK), BF16),
                   jax.ShapeDtypeStruct((b, N_KV, N_HALF, HEAD_DIM), BF16),
                   jax.ShapeDtypeStruct((b, N_KV, HEAD_DIM, N_HALF), BF16)],
        compiler_params=_cparams(("parallel", "arbitrary"), 40),
        name="nsa_prep",
    )(z, z, z, z, z, z, rope_c, rope_a, rope_b,
      pe_k, w1(w1_k).astype(BF16), w2_k.astype(BF16), pe_v, w1(w1_v).astype(BF16), w2_v.astype(BF16))


Q_SCALE = SCALE * 1.4426950408889634


def _scores_t(k, q):
    return lax.dot_general(k, q, (((1,), (1,)), ((), ())), preferred_element_type=F32)


def _online_attend(q, k_ref, vt_ref, bias_ref, tile, n_tiles):
    rows = q.shape[0]
    m = jnp.full((1, rows), NEG, F32)
    l = jnp.zeros((1, rows), F32)
    acc = jnp.zeros((HEAD_DIM, rows), F32)
    for kt in range(n_tiles):
        bias = bias_ref[kt * tile:(kt + 1) * tile, :]
        s = (_scores_t(k_ref[0, 0, kt * tile:(kt + 1) * tile, :], q)
             + jnp.concatenate([bias] * (rows // bias.shape[1]), axis=1))
        m_new = jnp.maximum(m, jnp.max(s, axis=0, keepdims=True))
        alpha = jnp.exp2(m - m_new)
        p = jnp.exp2(s - m_new)
        l = alpha * l + jnp.sum(p, axis=0, keepdims=True)
        acc = alpha * acc + jnp.dot(vt_ref[0, 0, kt], p.astype(BF16), preferred_element_type=F32)
        m = m_new
    return acc / l


def _nsa_attn_kernel(q_ref, zg_ref, rc_ref, ra_ref, rb_ref, ksr_ref, vst_ref, kwr_ref, vwt_ref,
                     kcmp_ref, vcmpt_ref, ovt_ref, exp_ref, o_ref, bias_scr, gate_scr, osel_scr, *, tq, seq):
    grp = pl.program_id(1)
    t0 = pl.program_id(2) * tq
    rows = HPG * tq
    heads = lambda x: jnp.concatenate([x] * HPG, axis=1)

    c, a, b = rc_ref[0], ra_ref[0], rb_ref[0]
    q_plain, q_rot = [], []
    for h in range(HPG):
        qh = q_ref[0, :, h * HEAD_DIM:(h + 1) * HEAD_DIM] * Q_SCALE
        q_plain.append(qh.astype(BF16))
        q_rot.append(_rope(qh, c, a, b).astype(BF16))
    q_plain = jnp.concatenate(q_plain, axis=0)
    q_rot = jnp.concatenate(q_rot, axis=0)

    tpos = t0 + lax.broadcasted_iota(jnp.int32, (1, tq), 1)

    s_c = _scores_t(kcmp_ref[0, 0], q_plain)
    n_idx = lax.broadcasted_iota(jnp.int32, (N_HALF, rows), 0)
    cmask = n_idx * CMP_STRIDE + (CMP_LEN - 1) <= heads(tpos)
    s_c = jnp.where(cmask, s_c, NEG)
    e = jnp.exp2(s_c - jnp.max(s_c, axis=0, keepdims=True))
    p_c = jnp.where(cmask, e / jnp.sum(e, axis=0, keepdims=True), 0.0)
    o_c = jnp.dot(vcmpt_ref[0, 0], p_c.astype(BF16), preferred_element_type=F32)

    span = WINDOW + tq
    k0 = pl.multiple_of(jnp.maximum(t0 - WINDOW, 0), WIN_TK)
    kp = k0 + lax.broadcasted_iota(jnp.int32, (span, tq), 0)
    w_bias = jnp.where((kp <= tpos) & (kp > tpos - WINDOW), 0.0, NEG)
    s_w = _scores_t(kwr_ref[0, 0, pl.ds(k0, span), :], q_rot) + heads(w_bias)
    p_w = jnp.exp2(s_w - jnp.max(s_w, axis=0, keepdims=True))
    l_w = jnp.sum(p_w, axis=0, keepdims=True)
    p_w = p_w.astype(BF16)
    o_w = jnp.zeros((HEAD_DIM, rows), F32)
    for i in range(span // WIN_TK):
        o_w = o_w + jnp.dot(vwt_ref[0, 0, k0 // WIN_TK + i], p_w[i * WIN_TK:(i + 1) * WIN_TK, :],
                            preferred_element_type=F32)
    o_w = o_w / l_w

    p_sum = p_c[:, 0:tq]
    for h in range(1, HPG):
        p_sum = p_sum + p_c[:, h * tq:(h + 1) * tq]
    p_hi = p_sum.astype(BF16)
    p_lo = (p_sum - p_hi.astype(F32)).astype(BF16)
    ovt = ovt_ref[...]
    imp = (jnp.dot(ovt, p_hi, preferred_element_type=F32)
           + jnp.dot(ovt, p_lo, preferred_element_type=F32))
    n_blk = seq // SEL_BLOCK
    j_idx = lax.broadcasted_iota(jnp.int32, (n_blk, tq), 0)
    cur = tpos // SEL_BLOCK
    forced = (j_idx == 0) | (j_idx == cur) | (j_idx == cur - 1)
    imp = imp + jnp.where(forced, FORCE_BONUS, 0.0)
    imp = jnp.where(j_idx * SEL_BLOCK <= tpos, imp, NEG)
    rank = jnp.zeros((n_blk, tq), jnp.int32)
    for i in range(n_blk):
        vi = imp[i:i + 1, :]
        ahead = (vi > imp) | ((vi == imp) & (j_idx > i))
        rank = rank + ahead.astype(jnp.int32)
    sel = (rank < min(SEL_TOPK, n_blk)).astype(BF16)

    sel_keys = jnp.dot(exp_ref[...], sel, preferred_element_type=F32)
    kpos = lax.broadcasted_iota(jnp.int32, (seq, tq), 0)
    bias_scr[...] = jnp.where((sel_keys > 0.5) & (kpos <= tpos), 0.0, NEG)

    n_tiles = (t0 + tq + SEL_TK - 1) // SEL_TK
    for n in range(1, seq // SEL_TK + 1):
        @pl.when(n_tiles == n)
        def _(n=n):
            osel_scr[...] = _online_attend(q_rot, ksr_ref, vst_ref, bias_scr, SEL_TK, n)
    o_s = osel_scr[...]

    gate_scr[...] = jax.nn.sigmoid(zg_ref[0]).T

    def gate(branch):
        return jnp.concatenate(
            [gate_scr[pl.ds(3 * (HPG * grp + h) + branch, 1), :] for h in range(HPG)], axis=1)

    o = gate(0) * o_c + gate(1) * o_s + gate(2) * o_w
    for h in range(HPG):
        o_ref[0, :, h * HEAD_DIM:(h + 1) * HEAD_DIM] = o[:, h * tq:(h + 1) * tq].T.astype(o_ref.dtype)


def _nsa_attn(z, rope_c, rope_a, rope_b, ksr, vst, kwr, vwt, kcmp, vcmpt, tq=256):
    b, s, _ = z.shape
    n_blk = s // SEL_BLOCK
    n = np.arange(N_HALF)[None, :]
    j = np.arange(n_blk)[:, None]
    ovt = ((n < N_HALF - 1) & (n * CMP_STRIDE < (j + 1) * SEL_BLOCK)
           & (n * CMP_STRIDE + CMP_LEN - 1 >= j * SEL_BLOCK))
    expand = (np.arange(s)[:, None] // SEL_BLOCK) == np.arange(n_blk)[None, :]
    gate_blk = NSA_MAIN // HEAD_DIM
    tab = pl.BlockSpec((1, tq, HEAD_DIM), lambda i, g, t: (i, t, 0))
    kv4 = lambda shape: pl.BlockSpec((1, 1) + shape, lambda i, g, t: (i, g, 0, 0))
    kv5 = lambda shape: pl.BlockSpec((1, 1) + shape, lambda i, g, t: (i, g, 0, 0, 0))
    return pl.pallas_call(
        functools.partial(_nsa_attn_kernel, tq=tq, seq=s),
        grid=(b, N_KV, s // tq),
        in_specs=[
            pl.BlockSpec((1, tq, HPG * HEAD_DIM), lambda i, g, t: (i, t, g)),
            pl.BlockSpec((1, tq, HEAD_DIM), lambda i, g, t: (i, t, gate_blk)),
            tab, tab, tab,
            kv4((s, HEAD_DIM)), kv5((s // SEL_TK, HEAD_DIM, SEL_TK)),
            kv4((s, HEAD_DIM)), kv5((s // WIN_TK, HEAD_DIM, WIN_TK)),
            kv4((N_HALF, HEAD_DIM)), kv4((HEAD_DIM, N_HALF)),
            pl.BlockSpec((n_blk, N_HALF), lambda i, g, t: (0, 0)),
            pl.BlockSpec((s, n_blk), lambda i, g, t: (0, 0)),
        ],
        out_specs=pl.BlockSpec((1, tq, HPG * HEAD_DIM), lambda i, g, t: (i, t, g)),
        out_shape=jax.ShapeDtypeStruct((b, s, D_MODEL), BF16),
        scratch_shapes=[pltpu.VMEM((s, tq), F32), pltpu.VMEM((HEAD_DIM, tq), F32),
                        pltpu.VMEM((HEAD_DIM, HPG * tq), F32)],
        compiler_params=_cparams(("parallel", "parallel", "arbitrary"), 40),
        name="nsa_attn",
    )(z, z, rope_c, rope_a, rope_b, ksr, vst, kwr, vwt, kcmp, vcmpt,
      jnp.asarray(ovt, BF16), jnp.asarray(expand, BF16))


def _rope_tables(positions):
    half = ROT_DIM // 2
    inv = ROPE_THETA ** (-jnp.arange(0, ROT_DIM, 2, dtype=F32) / ROT_DIM)
    ang = positions.astype(F32)[..., None] * inv
    cos, sin = jnp.cos(ang), jnp.sin(ang)
    zeros = lambda w: jnp.zeros(cos.shape[:-1] + (w,), F32)
    c = jnp.concatenate([cos, cos, jnp.ones(cos.shape[:-1] + (HEAD_DIM - ROT_DIM,), F32)], axis=-1)
    a = jnp.concatenate([-sin, zeros(HEAD_DIM - half)], axis=-1)
    b = jnp.concatenate([zeros(half), sin, zeros(HEAD_DIM - ROT_DIM)], axis=-1)
    return c, a, b


def kernel(x, positions, l0_ffn1_norm, l0_ffn1_w_gate, l0_ffn1_w_up, l0_ffn1_w_down, l0_mix_norm, l0_w_in, l0_conv_w, l0_conv_b, l0_conv_ln_g, l0_conv_ln_b, l0_gmlp_ln_g, l0_gmlp_ln_b, l0_gmlp_ws, l0_gmlp_bs, l0_w_out, l0_ffn2_norm, l0_ffn2_w_gate, l0_ffn2_w_up, l0_ffn2_w_down, l1_ffn1_norm, l1_ffn1_w_gate, l1_ffn1_w_up, l1_ffn1_w_down, l1_mix_norm, l1_w_in, l1_cmp_pe_k, l1_cmp_w1_k, l1_cmp_w2_k, l1_cmp_pe_v, l1_cmp_w1_v, l1_cmp_w2_v, l1_w_out, l1_ffn2_norm, l1_ffn2_w_gate, l1_ffn2_w_up, l1_ffn2_w_down, final_norm):
    b, s, d = x.shape
    m = b * s
    bf = lambda w: w.astype(BF16)
    x2 = x.reshape(m, d)

    x2 = _ffn(x2, l0_ffn1_norm, l0_ffn1_w_gate, l0_ffn1_w_up, l0_ffn1_w_down)
    z = _norm_matmul(x2, l0_mix_norm, bf(l0_w_in)).reshape(b, s, -1)
    cat = _l0_mid(z, l0_conv_w, l0_conv_b, l0_conv_ln_g, l0_conv_ln_b,
                  l0_gmlp_ln_g, l0_gmlp_ln_b, l0_gmlp_ws, l0_gmlp_bs)
    x2 = _matmul_res(cat.reshape(m, -1), bf(l0_w_out), x2)
    x2 = _ffn(x2, l0_ffn2_norm, l0_ffn2_w_gate, l0_ffn2_w_up, l0_ffn2_w_down)

    x2 = _ffn(x2, l1_ffn1_norm, l1_ffn1_w_gate, l1_ffn1_w_up, l1_ffn1_w_down)
    w_in = jnp.pad(bf(l1_w_in), ((0, 0), (0, NSA_PAD - NSA_MAIN - NSA_GATES)))
    z = _norm_matmul(x2, l1_mix_norm, w_in).reshape(b, s, -1)
    rope_c, rope_a, rope_b = _rope_tables(positions)
    ksr, vst, kwr, vwt, kcmp, vcmpt = _nsa_prep(
        z, rope_c, rope_a, rope_b, l1_cmp_pe_k, l1_cmp_w1_k, l1_cmp_w2_k,
        l1_cmp_pe_v, l1_cmp_w1_v, l1_cmp_w2_v)
    o = _nsa_attn(z, rope_c, rope_a, rope_b, ksr, vst, kwr, vwt, kcmp, vcmpt)
    x2 = _matmul_res(o.reshape(m, -1), bf(l1_w_out), x2)
    x2 = _ffn(x2, l1_ffn2_norm, l1_ffn2_w_gate, l1_ffn2_w_up, l1_ffn2_w_down,
              final_w=final_norm)
    return x2.reshape(b, s, d)
```

```python
import functools

import numpy as np
import jax
import jax.numpy as jnp
from jax import lax
from jax.experimental import pallas as pl
from jax.experimental.pallas import tpu as pltpu

F32 = jnp.float32
BF16 = jnp.bfloat16

D_MODEL = 2048
D_FF = 5632
EPS = 1e-6
CONV_CH = 1024
CONV_WIDTH = 31
GMLP_CH = 1024
GMLP_GROUPS = 8
GMLP_GDIM = 128
GMLP_CHUNK = 128
HEAD_DIM = 128
N_HEADS = 16
N_KV = 4
HPG = 4
KV_W = N_KV * HEAD_DIM
ROT_DIM = 32
ROPE_THETA = 500000.0
CMP_LEN = 32
CMP_STRIDE = 16
CMP_HIDDEN = 256
SEL_BLOCK = 64
SEL_TOPK = 16
WINDOW = 512
FORCE_BONUS = 1e3
NEG = -1e30
NSA_MAIN = D_MODEL + 6 * KV_W
NSA_GATES = 3 * N_HEADS
NSA_PAD = 5632
SCALE = HEAD_DIM ** -0.5

MIB = 1024 * 1024


def _cparams(semantics, vmem_mib):
    return pltpu.CompilerParams(dimension_semantics=semantics,
                                vmem_limit_bytes=vmem_mib * MIB)


def _rms_scale(x):
    return x * lax.rsqrt(jnp.mean(x * x, axis=-1, keepdims=True) + EPS)


DOWN_TN = 512


def _ffn_kernel(x_ref, nw_ref, wg_ref, wu_ref, wd_ref, *rest, final):
    if final:
        fn_ref, o_ref, h_scr = rest
    else:
        o_ref, h_scr = rest
    f = pl.program_id(1)

    @pl.when(f == 0)
    def _():
        h_scr[...] = (_rms_scale(x_ref[...]) * nw_ref[...]).astype(BF16)
        o_ref[...] = jnp.zeros(o_ref.shape, F32)

    h = h_scr[...]
    g = jnp.dot(h, wg_ref[...].astype(BF16), preferred_element_type=F32)
    u = jnp.dot(h, wu_ref[...].astype(BF16), preferred_element_type=F32)
    a = (g * jax.nn.sigmoid(g) * u).astype(BF16)
    for c in range(0, o_ref.shape[1], DOWN_TN):
        o_ref[:, c:c + DOWN_TN] += jnp.dot(a, wd_ref[:, c:c + DOWN_TN].astype(BF16),
                                           preferred_element_type=F32)

    @pl.when(f == pl.num_programs(1) - 1)
    def _():
        y = x_ref[...] + 0.5 * o_ref[...]
        if final:
            y = _rms_scale(y) * fn_ref[...]
        o_ref[...] = y


def _ffn(x2d, norm_w, wg, wu, wd, final_w=None, tm=1024, tf=256):
    m, d = x2d.shape
    dff = wg.shape[1]
    final = final_w is not None
    in_specs = [
        pl.BlockSpec((tm, d), lambda i, j: (i, 0)),
        pl.BlockSpec((1, d), lambda i, j: (0, 0)),
        pl.BlockSpec((d, tf), lambda i, j: (0, j)),
        pl.BlockSpec((d, tf), lambda i, j: (0, j)),
        pl.BlockSpec((tf, d), lambda i, j: (j, 0)),
    ]
    args = [x2d, norm_w.reshape(1, d), wg, wu, wd]
    if final:
        in_specs.append(pl.BlockSpec((1, d), lambda i, j: (0, 0)))
        args.append(final_w.reshape(1, d))
    return pl.pallas_call(
        functools.partial(_ffn_kernel, final=final),
        grid=(m // tm, dff // tf),
        in_specs=in_specs,
        out_specs=pl.BlockSpec((tm, d), lambda i, j: (i, 0)),
        out_shape=jax.ShapeDtypeStruct((m, d), F32),
        scratch_shapes=[pltpu.VMEM((tm, d), BF16)],
        compiler_params=_cparams(("parallel", "arbitrary"), 58),
        name="ffn",
    )(*args)


def _norm_matmul_kernel(x_ref, nw_ref, w_ref, o_ref, h_scr):
    @pl.when(pl.program_id(1) == 0)
    def _():
        h_scr[...] = (_rms_scale(x_ref[...]) * nw_ref[...]).astype(BF16)

    o_ref[...] = jnp.dot(h_scr[...], w_ref[...], preferred_element_type=F32)


def _norm_matmul(x2d, norm_w, w, tm=1024, tn=512):
    m, d = x2d.shape
    n = w.shape[1]
    return pl.pallas_call(
        _norm_matmul_kernel,
        grid=(m // tm, n // tn),
        in_specs=[
            pl.BlockSpec((tm, d), lambda i, j: (i, 0)),
            pl.BlockSpec((1, d), lambda i, j: (0, 0)),
            pl.BlockSpec((d, tn), lambda i, j: (0, j)),
        ],
        out_specs=pl.BlockSpec((tm, tn), lambda i, j: (i, j)),
        out_shape=jax.ShapeDtypeStruct((m, n), F32),
        scratch_shapes=[pltpu.VMEM((tm, d), BF16)],
        compiler_params=_cparams(("parallel", "arbitrary"), 40),
        name="norm_matmul",
    )(x2d, norm_w.reshape(1, d), w)


def _matmul_res_kernel(a_ref, w_ref, r_ref, o_ref):
    o_ref[...] = r_ref[...] + jnp.dot(a_ref[...], w_ref[...], preferred_element_type=F32)


def _matmul_res(a, w, res, tm=512, tn=1024):
    m, k = a.shape
    n = w.shape[1]
    return pl.pallas_call(
        _matmul_res_kernel,
        grid=(m // tm, n // tn),
        in_specs=[
            pl.BlockSpec((tm, k), lambda i, j: (i, 0)),
            pl.BlockSpec((k, tn), lambda i, j: (0, j)),
            pl.BlockSpec((tm, tn), lambda i, j: (i, j)),
        ],
        out_specs=pl.BlockSpec((tm, tn), lambda i, j: (i, j)),
        out_shape=jax.ShapeDtypeStruct((m, n), F32),
        compiler_params=_cparams(("parallel", "arbitrary"), 32),
        name="matmul_res",
    )(a, w, res)


CONV_HALO = 32
SUBLANES = 8


def _layernorm(x, g, b):
    mu = jnp.mean(x, axis=-1, keepdims=True)
    xc = x - mu
    var = jnp.mean(xc * xc, axis=-1, keepdims=True)
    return xc * lax.rsqrt(var + EPS) * g + b


def _l0_mid_kernel(av_ref, ag_ref, bu_ref, bv_ref, cw_ref, cb_ref, clg_ref, clb_ref,
                   glg_ref, glb_ref, ws_ref, bst_ref, o_ref, abuf, shifted, *, ts):
    s = pl.program_id(1)

    @pl.when(s == 0)
    def _():
        abuf[0:CONV_HALO, :] = jnp.zeros((CONV_HALO, CONV_CH), F32)

    abuf[CONV_HALO:CONV_HALO + ts, :] = av_ref[0] * jax.nn.sigmoid(ag_ref[0])
    span = CONV_HALO + ts - SUBLANES
    for r in range(1, SUBLANES):
        shifted[r - 1] = abuf[r:r + span, :]
    base = CONV_HALO - (CONV_WIDTH - 1)
    acc = jnp.zeros((ts, CONV_CH), F32)
    for j in range(CONV_WIDTH):
        q, r = divmod(base + j, SUBLANES)
        src = abuf if r == 0 else shifted.at[r - 1]
        rows = src[SUBLANES * q:SUBLANES * q + ts, :]
        acc = acc + rows * cw_ref[j:j + 1, :]
    acc = acc + cb_ref[...]
    a = _layernorm(acc, clg_ref[...], clb_ref[...])
    o_ref[0, :, 0:CONV_CH] = (a * jax.nn.sigmoid(a)).astype(o_ref.dtype)
    abuf[0:CONV_HALO, :] = abuf[ts:ts + CONV_HALO, :]

    row = lax.broadcasted_iota(jnp.int32, (GMLP_CHUNK, GMLP_CHUNK), 0)
    col = lax.broadcasted_iota(jnp.int32, (GMLP_CHUNK, GMLP_CHUNK), 1)
    tri = col <= row
    for g in range(GMLP_GROUPS):
        lo = g * GMLP_GDIM
        w_g = jnp.where(tri, ws_ref[g], 0.0).astype(BF16)
        bias_g = bst_ref[:, g:g + 1]
        ln_g = glg_ref[:, lo:lo + GMLP_GDIM]
        ln_b = glb_ref[:, lo:lo + GMLP_GDIM]
        for c in range(ts // GMLP_CHUNK):
            r0 = c * GMLP_CHUNK
            u = jax.nn.gelu(bu_ref[0, r0:r0 + GMLP_CHUNK, lo:lo + GMLP_GDIM])
            v = jax.nn.gelu(bv_ref[0, r0:r0 + GMLP_CHUNK, lo:lo + GMLP_GDIM])
            v = _layernorm(v, ln_g, ln_b)
            sv = jnp.dot(w_g, v.astype(BF16), preferred_element_type=F32) + bias_g
            o_ref[0, r0:r0 + GMLP_CHUNK, CONV_CH + lo:CONV_CH + lo + GMLP_GDIM] = (
                (u * sv).astype(o_ref.dtype))


def _l0_mid(z, conv_w, conv_b, conv_ln_g, conv_ln_b, gmlp_ln_g, gmlp_ln_b, gmlp_ws, gmlp_bs, ts=256):
    b, s, _ = z.shape
    vec = lambda a: a.reshape(1, -1)
    const2 = lambda shape: pl.BlockSpec(shape, lambda i, j: (0, 0))
    zspec = lambda k: pl.BlockSpec((1, ts, CONV_CH), lambda i, j, k=k: (i, j, k))
    return pl.pallas_call(
        functools.partial(_l0_mid_kernel, ts=ts),
        grid=(b, s // ts),
        in_specs=[
            zspec(0), zspec(1), zspec(2), zspec(3),
            const2((CONV_WIDTH, CONV_CH)),
            const2((1, CONV_CH)), const2((1, CONV_CH)), const2((1, CONV_CH)),
            const2((1, GMLP_CH)), const2((1, GMLP_CH)),
            pl.BlockSpec((GMLP_GROUPS, GMLP_CHUNK, GMLP_CHUNK), lambda i, j: (0, 0, 0)),
            const2((GMLP_CHUNK, GMLP_GROUPS)),
        ],
        out_specs=pl.BlockSpec((1, ts, CONV_CH + GMLP_CH), lambda i, j: (i, j, 0)),
        out_shape=jax.ShapeDtypeStruct((b, s, CONV_CH + GMLP_CH), BF16),
        scratch_shapes=[pltpu.VMEM((CONV_HALO + ts, CONV_CH), F32),
                        pltpu.VMEM((SUBLANES - 1, CONV_HALO + ts - SUBLANES, CONV_CH), F32)],
        compiler_params=_cparams(("parallel", "arbitrary"), 32),
        name="l0_mid",
    )(z, z, z, z, conv_w.reshape(CONV_WIDTH, CONV_CH), vec(conv_b), vec(conv_ln_g), vec(conv_ln_b),
      vec(gmlp_ln_g), vec(gmlp_ln_b), gmlp_ws, gmlp_bs.T)


SEL_TK = 512
WIN_TK = 128
N_HALF = 128


def _rope(x, c, a, b):
    return x * c + pltpu.roll(x, HEAD_DIM - ROT_DIM // 2, 1) * a + pltpu.roll(x, ROT_DIM // 2, 1) * b


def _compress(k_ref, pe_ref, w1_ref, w2_ref):
    half = CMP_LEN // 2
    p = jnp.zeros((N_HALF, CMP_HIDDEN), F32)
    q = jnp.zeros((N_HALF, CMP_HIDDEN), F32)
    for l in range(half):
        rows = k_ref[0, pl.ds(l, N_HALF, stride=CMP_STRIDE), :]
        p = p + jnp.dot((rows + pe_ref[l:l + 1, :]).astype(BF16), w1_ref[l],
                        preferred_element_type=F32)
        q = q + jnp.dot((rows + pe_ref[half + l:half + l + 1, :]).astype(BF16), w1_ref[half + l],
                        preferred_element_type=F32)
    hid = p + pltpu.roll(q, N_HALF - 1, 0)
    return jnp.dot(jax.nn.gelu(hid).astype(BF16), w2_ref[...], preferred_element_type=F32)


def _nsa_prep_kernel(kc_ref, vc_ref, ks_ref, vs_ref, kw_ref, vw_ref, rc_ref, ra_ref, rb_ref,
                     pek_ref, w1k_ref, w2k_ref, pev_ref, w1v_ref, w2v_ref,
                     ksr_ref, vst_ref, kwr_ref, vwt_ref, kcmp_ref, vcmpt_ref):
    c, a, b = rc_ref[0], ra_ref[0], rb_ref[0]
    ksr_ref[0, 0] = _rope(ks_ref[0], c, a, b).astype(BF16)
    kwr_ref[0, 0] = _rope(kw_ref[0], c, a, b).astype(BF16)
    s = vs_ref.shape[1]
    for t in range(s // SEL_TK):
        vst_ref[0, 0, t] = vs_ref[0, t * SEL_TK:(t + 1) * SEL_TK, :].T.astype(BF16)
    for t in range(s // WIN_TK):
        vwt_ref[0, 0, t] = vw_ref[0, t * WIN_TK:(t + 1) * WIN_TK, :].T.astype(BF16)
    kcmp_ref[0, 0] = _compress(kc_ref, pek_ref, w1k_ref, w2k_ref).astype(BF16)
    vcmpt_ref[0, 0] = _compress(vc_ref, pev_ref, w1v_ref, w2v_ref).T.astype(BF16)


def _nsa_prep(z, rope_c, rope_a, rope_b, pe_k, w1_k, w2_k, pe_v, w1_v, w2_v):
    b, s, _ = z.shape
    first = D_MODEL // HEAD_DIM
    zspec = lambda k: pl.BlockSpec((1, s, HEAD_DIM), lambda i, g, k=k: (i, 0, first + k * N_KV + g))
    tab = pl.BlockSpec((1, s, HEAD_DIM), lambda i, g: (i, 0, 0))
    c2 = lambda shape: pl.BlockSpec(shape, lambda i, g: (0, 0))
    c3 = lambda shape: pl.BlockSpec(shape, lambda i, g: (0, 0, 0))
    out4 = lambda shape: pl.BlockSpec((1, 1) + shape, lambda i, g: (i, g, 0, 0))
    out5 = lambda shape: pl.BlockSpec((1, 1) + shape, lambda i, g: (i, g, 0, 0, 0))
    w1 = lambda w: w.reshape(CMP_LEN, HEAD_DIM, CMP_HIDDEN)
    return pl.pallas_call(
        _nsa_prep_kernel,
        grid=(b, N_KV),
        in_specs=[zspec(0), zspec(1), zspec(2), zspec(3), zspec(4), zspec(5), tab, tab, tab,
                  c2((CMP_LEN, HEAD_DIM)), c3((CMP_LEN, HEAD_DIM, CMP_HIDDEN)), c2((CMP_HIDDEN, HEAD_DIM)),
                  c2((CMP_LEN, HEAD_DIM)), c3((CMP_LEN, HEAD_DIM, CMP_HIDDEN)), c2((CMP_HIDDEN, HEAD_DIM))],
        out_specs=[out4((s, HEAD_DIM)), out5((s // SEL_TK, HEAD_DIM, SEL_TK)),
                   out4((s, HEAD_DIM)), out5((s // WIN_TK, HEAD_DIM, WIN_TK)),
                   out4((N_HALF, HEAD_DIM)), out4((HEAD_DIM, N_HALF))],
        out_shape=[jax.ShapeDtypeStruct((b, N_KV, s, HEAD_DIM), BF16),
                   jax.ShapeDtypeStruct((b, N_KV, s // SEL_TK, HEAD_DIM, SEL_TK), BF16),
                   jax.ShapeDtypeStruct((b, N_KV, s, HEAD_DIM), BF16),
                   jax.ShapeDtypeStruct((b, N_KV, s // WIN_TK, HEAD_DIM, WIN_TK), BF16),
                   jax.ShapeDtypeStruct((b, N_KV, N_HALF, HEAD_DIM), BF16),
                   jax.ShapeDtypeStruct((b, N_KV, HEAD_DIM, N_HALF), BF16)],
        compiler_params=_cparams(("parallel", "arbitrary"), 40),
        name="nsa_prep",
    )(z, z, z, z, z, z, rope_c, rope_a, rope_b,
      pe_k, w1(w1_k).astype(BF16), w2_k.astype(BF16), pe_v, w1(w1_v).astype(BF16), w2_v.astype(BF16))


Q_SCALE = SCALE * 1.4426950408889634


def _scores_t(k, q):
    return lax.dot_general(k, q, (((1,), (1,)), ((), ())), preferred_element_type=F32)


def _online_attend(q, k_ref, vt_ref, bias_ref, tile, n_tiles):
    rows = q.shape[0]
    m = jnp.full((1, rows), NEG, F32)
    l = jnp.zeros((1, rows), F32)
    acc = jnp.zeros((HEAD_DIM, rows), F32)
    for kt in range(n_tiles):
        bias = bias_ref[kt * tile:(kt + 1) * tile, :]
        s = (_scores_t(k_ref[0, 0, kt * tile:(kt + 1) * tile, :], q)
             + jnp.concatenate([bias] * (rows // bias.shape[1]), axis=1))
        m_new = jnp.maximum(m, jnp.max(s, axis=0, keepdims=True))
        alpha = jnp.exp2(m - m_new)
        p = jnp.exp2(s - m_new)
        l = alpha * l + jnp.sum(p, axis=0, keepdims=True)
        acc = alpha * acc + jnp.dot(vt_ref[0, 0, kt], p.astype(BF16), preferred_element_type=F32)
        m = m_new
    return acc / l


def _nsa_attn_kernel(q_ref, zg_ref, rc_ref, ra_ref, rb_ref, ksr_ref, vst_ref, kwr_ref, vwt_ref,
                     kcmp_ref, vcmpt_ref, ovt_ref, exp_ref, o_ref, bias_scr, gate_scr, osel_scr, *, tq, seq):
    grp = pl.program_id(1)
    t0 = pl.program_id(2) * tq
    rows = HPG * tq
    heads = lambda x: jnp.concatenate([x] * HPG, axis=1)

    c, a, b = rc_ref[0], ra_ref[0], rb_ref[0]
    q_plain, q_rot = [], []
    for h in range(HPG):
        qh = q_ref[0, :, h * HEAD_DIM:(h + 1) * HEAD_DIM] * Q_SCALE
        q_plain.append(qh.astype(BF16))
        q_rot.append(_rope(qh, c, a, b).astype(BF16))
    q_plain = jnp.concatenate(q_plain, axis=0)
    q_rot = jnp.concatenate(q_rot, axis=0)

    tpos = t0 + lax.broadcasted_iota(jnp.int32, (1, tq), 1)

    s_c = _scores_t(kcmp_ref[0, 0], q_plain)
    n_idx = lax.broadcasted_iota(jnp.int32, (N_HALF, rows), 0)
    cmask = n_idx * CMP_STRIDE + (CMP_LEN - 1) <= heads(tpos)
    s_c = jnp.where(cmask, s_c, NEG)
    e = jnp.exp2(s_c - jnp.max(s_c, axis=0, keepdims=True))
    p_c = jnp.where(cmask, e / jnp.sum(e, axis=0, keepdims=True), 0.0)
    o_c = jnp.dot(vcmpt_ref[0, 0], p_c.astype(BF16), preferred_element_type=F32)

    span = WINDOW + tq
    k0 = pl.multiple_of(jnp.maximum(t0 - WINDOW, 0), WIN_TK)
    kp = k0 + lax.broadcasted_iota(jnp.int32, (span, tq), 0)
    w_bias = jnp.where((kp <= tpos) & (kp > tpos - WINDOW), 0.0, NEG)
    s_w = _scores_t(kwr_ref[0, 0, pl.ds(k0, span), :], q_rot) + heads(w_bias)
    p_w = jnp.exp2(s_w - jnp.max(s_w, axis=0, keepdims=True))
    l_w = jnp.sum(p_w, axis=0, keepdims=True)
    p_w = p_w.astype(BF16)
    o_w = jnp.zeros((HEAD_DIM, rows), F32)
    for i in range(span // WIN_TK):
        o_w = o_w + jnp.dot(vwt_ref[0, 0, k0 // WIN_TK + i], p_w[i * WIN_TK:(i + 1) * WIN_TK, :],
                            preferred_element_type=F32)
    o_w = o_w / l_w

    p_sum = p_c[:, 0:tq]
    for h in range(1, HPG):
        p_sum = p_sum + p_c[:, h * tq:(h + 1) * tq]
    p_hi = p_sum.astype(BF16)
    p_lo = (p_sum - p_hi.astype(F32)).astype(BF16)
    ovt = ovt_ref[...]
    imp = (jnp.dot(ovt, p_hi, preferred_element_type=F32)
           + jnp.dot(ovt, p_lo, preferred_element_type=F32))
    n_blk = seq // SEL_BLOCK
    j_idx = lax.broadcasted_iota(jnp.int32, (n_blk, tq), 0)
    cur = tpos // SEL_BLOCK
    forced = (j_idx == 0) | (j_idx == cur) | (j_idx == cur - 1)
    imp = imp + jnp.where(forced, FORCE_BONUS, 0.0)
    imp = jnp.where(j_idx * SEL_BLOCK <= tpos, imp, NEG)
    rank = jnp.zeros((n_blk, tq), jnp.int32)
    for i in range(n_blk):
        vi = imp[i:i + 1, :]
        ahead = (vi > imp) | ((vi == imp) & (j_idx > i))
        rank = rank + ahead.astype(jnp.int32)
    sel = (rank < min(SEL_TOPK, n_blk)).astype(BF16)

    sel_keys = jnp.dot(exp_ref[...], sel, preferred_element_type=F32)
    kpos = lax.broadcasted_iota(jnp.int32, (seq, tq), 0)
    bias_scr[...] = jnp.where((sel_keys > 0.5) & (kpos <= tpos), 0.0, NEG)

    n_tiles = (t0 + tq + SEL_TK - 1) // SEL_TK
    for n in range(1, seq // SEL_TK + 1):
        @pl.when(n_tiles == n)
        def _(n=n):
            osel_scr[...] = _online_attend(q_rot, ksr_ref, vst_ref, bias_scr, SEL_TK, n)
    o_s = osel_scr[...]

    gate_scr[...] = jax.nn.sigmoid(zg_ref[0]).T

    def gate(branch):
        return jnp.concatenate(
            [gate_scr[pl.ds(3 * (HPG * grp + h) + branch, 1), :] for h in range(HPG)], axis=1)

    o = gate(0) * o_c + gate(1) * o_s + gate(2) * o_w
    for h in range(HPG):
        o_ref[0, :, h * HEAD_DIM:(h + 1) * HEAD_DIM] = o[:, h * tq:(h + 1) * tq].T.astype(o_ref.dtype)


def _nsa_attn(z, rope_c, rope_a, rope_b, ksr, vst, kwr, vwt, kcmp, vcmpt, tq=256):
    b, s, _ = z.shape
    n_blk = s // SEL_BLOCK
    n = np.arange(N_HALF)[None, :]
    j = np.arange(n_blk)[:, None]
    ovt = ((n < N_HALF - 1) & (n * CMP_STRIDE < (j + 1) * SEL_BLOCK)
           & (n * CMP_STRIDE + CMP_LEN - 1 >= j * SEL_BLOCK))
    expand = (np.arange(s)[:, None] // SEL_BLOCK) == np.arange(n_blk)[None, :]
    gate_blk = NSA_MAIN // HEAD_DIM
    tab = pl.BlockSpec((1, tq, HEAD_DIM), lambda i, g, t: (i, t, 0))
    kv4 = lambda shape: pl.BlockSpec((1, 1) + shape, lambda i, g, t: (i, g, 0, 0))
    kv5 = lambda shape: pl.BlockSpec((1, 1) + shape, lambda i, g, t: (i, g, 0, 0, 0))
    return pl.pallas_call(
        functools.partial(_nsa_attn_kernel, tq=tq, seq=s),
        grid=(b, N_KV, s // tq),
        in_specs=[
            pl.BlockSpec((1, tq, HPG * HEAD_DIM), lambda i, g, t: (i, t, g)),
            pl.BlockSpec((1, tq, HEAD_DIM), lambda i, g, t: (i, t, gate_blk)),
            tab, tab, tab,
            kv4((s, HEAD_DIM)), kv5((s // SEL_TK, HEAD_DIM, SEL_TK)),
            kv4((s, HEAD_DIM)), kv5((s // WIN_TK, HEAD_DIM, WIN_TK)),
            kv4((N_HALF, HEAD_DIM)), kv4((HEAD_DIM, N_HALF)),
            pl.BlockSpec((n_blk, N_HALF), lambda i, g, t: (0, 0)),
            pl.BlockSpec((s, n_blk), lambda i, g, t: (0, 0)),
        ],
        out_specs=pl.BlockSpec((1, tq, HPG * HEAD_DIM), lambda i, g, t: (i, t, g)),
        out_shape=jax.ShapeDtypeStruct((b, s, D_MODEL), BF16),
        scratch_shapes=[pltpu.VMEM((s, tq), F32), pltpu.VMEM((HEAD_DIM, tq), F32),
                        pltpu.VMEM((HEAD_DIM, HPG * tq), F32)],
        compiler_params=_cparams(("parallel", "parallel", "arbitrary"), 40),
        name="nsa_attn",
    )(z, z, rope_c, rope_a, rope_b, ksr, vst, kwr, vwt, kcmp, vcmpt,
      jnp.asarray(ovt, BF16), jnp.asarray(expand, BF16))


def _rope_tables(positions):
    half = ROT_DIM // 2
    inv = ROPE_THETA ** (-jnp.arange(0, ROT_DIM, 2, dtype=F32) / ROT_DIM)
    ang = positions.astype(F32)[..., None] * inv
    cos, sin = jnp.cos(ang), jnp.sin(ang)
    zeros = lambda w: jnp.zeros(cos.shape[:-1] + (w,), F32)
    c = jnp.concatenate([cos, cos, jnp.ones(cos.shape[:-1] + (HEAD_DIM - ROT_DIM,), F32)], axis=-1)
    a = jnp.concatenate([-sin, zeros(HEAD_DIM - half)], axis=-1)
    b = jnp.concatenate([zeros(half), sin, zeros(HEAD_DIM - ROT_DIM)], axis=-1)
    return c, a, b


def kernel(x, positions, l0_ffn1_norm, l0_ffn1_w_gate, l0_ffn1_w_up, l0_ffn1_w_down, l0_mix_norm, l0_w_in, l0_conv_w, l0_conv_b, l0_conv_ln_g, l0_conv_ln_b, l0_gmlp_ln_g, l0_gmlp_ln_b, l0_gmlp_ws, l0_gmlp_bs, l0_w_out, l0_ffn2_norm, l0_ffn2_w_gate, l0_ffn2_w_up, l0_ffn2_w_down, l1_ffn1_norm, l1_ffn1_w_gate, l1_ffn1_w_up, l1_ffn1_w_down, l1_mix_norm, l1_w_in, l1_cmp_pe_k, l1_cmp_w1_k, l1_cmp_w2_k, l1_cmp_pe_v, l1_cmp_w1_v, l1_cmp_w2_v, l1_w_out, l1_ffn2_norm, l1_ffn2_w_gate, l1_ffn2_w_up, l1_ffn2_w_down, final_norm):
    b, s, d = x.shape
    m = b * s
    bf = lambda w: w.astype(BF16)
    x2 = x.reshape(m, d)

    x2 = _ffn(x2, l0_ffn1_norm, l0_ffn1_w_gate, l0_ffn1_w_up, l0_ffn1_w_down)
    z = _norm_matmul(x2, l0_mix_norm, bf(l0_w_in)).reshape(b, s, -1)
    cat = _l0_mid(z, l0_conv_w, l0_conv_b, l0_conv_ln_g, l0_conv_ln_b,
                  l0_gmlp_ln_g, l0_gmlp_ln_b, l0_gmlp_ws, l0_gmlp_bs)
    x2 = _matmul_res(cat.reshape(m, -1), bf(l0_w_out), x2)
    x2 = _ffn(x2, l0_ffn2_norm, l0_ffn2_w_gate, l0_ffn2_w_up, l0_ffn2_w_down)

    x2 = _ffn(x2, l1_ffn1_norm, l1_ffn1_w_gate, l1_ffn1_w_up, l1_ffn1_w_down)
    w_in = jnp.pad(bf(l1_w_in), ((0, 0), (0, NSA_PAD - NSA_MAIN - NSA_GATES)))
    z = _norm_matmul(x2, l1_mix_norm, w_in).reshape(b, s, -1)
    rope_c, rope_a, rope_b = _rope_tables(positions)
    ksr, vst, kwr, vwt, kcmp, vcmpt = _nsa_prep(
        z, rope_c, rope_a, rope_b, l1_cmp_pe_k, l1_cmp_w1_k, l1_cmp_w2_k,
        l1_cmp_pe_v, l1_cmp_w1_v, l1_cmp_w2_v)
    o = _nsa_attn(z, rope_c, rope_a, rope_b, ksr, vst, kwr, vwt, kcmp, vcmpt)
    x2 = _matmul_res(o.reshape(m, -1), bf(l1_w_out), x2)
    x2 = _ffn(x2, l1_ffn2_norm, l1_ffn2_w_gate, l1_ffn2_w_up, l1_ffn2_w_down,
              final_w=final_norm)
    return x2.reshape(b, s, d)
```

```python
import functools

import numpy as np
import jax
import jax.numpy as jnp
from jax import lax
from jax.experimental import pallas as pl
from jax.experimental.pallas import tpu as pltpu

F32 = jnp.float32
BF16 = jnp.bfloat16

D_MODEL = 2048
D_FF = 5632
EPS = 1e-6
CONV_CH = 1024
CONV_WIDTH = 31
GMLP_CH = 1024
GMLP_GROUPS = 8
GMLP_GDIM = 128
GMLP_CHUNK = 128
HEAD_DIM = 128
N_HEADS = 16
N_KV = 4
HPG = 4
KV_W = N_KV * HEAD_DIM
ROT_DIM = 32
ROPE_THETA = 500000.0
CMP_LEN = 32
CMP_STRIDE = 16
CMP_HIDDEN = 256
SEL_BLOCK = 64
SEL_TOPK = 16
WINDOW = 512
FORCE_BONUS = 1e3
NEG = -1e30
NSA_MAIN = D_MODEL + 6 * KV_W
NSA_GATES = 3 * N_HEADS
NSA_PAD = 5632
SCALE = HEAD_DIM ** -0.5

MIB = 1024 * 1024


def _cparams(semantics, vmem_mib):
    return pltpu.CompilerParams(dimension_semantics=semantics,
                                vmem_limit_bytes=vmem_mib * MIB)


def _rms_scale(x):
    return x * lax.rsqrt(jnp.mean(x * x, axis=-1, keepdims=True) + EPS)


DOWN_TN = 512


def _ffn_kernel(x_ref, nw_ref, wg_ref, wu_ref, wd_ref, *rest, final):
    if final:
        fn_ref, o_ref, h_scr = rest
    else:
        o_ref, h_scr = rest
    f = pl.program_id(1)

    @pl.when(f == 0)
    def _():
        h_scr[...] = (_rms_scale(x_ref[...]) * nw_ref[...]).astype(BF16)
        o_ref[...] = jnp.zeros(o_ref.shape, F32)

    h = h_scr[...]
    g = jnp.dot(h, wg_ref[...].astype(BF16), preferred_element_type=F32)
    u = jnp.dot(h, wu_ref[...].astype(BF16), preferred_element_type=F32)
    a = (g * jax.nn.sigmoid(g) * u).astype(BF16)
    for c in range(0, o_ref.shape[1], DOWN_TN):
        o_ref[:, c:c + DOWN_TN] += jnp.dot(a, wd_ref[:, c:c + DOWN_TN].astype(BF16),
                                           preferred_element_type=F32)

    @pl.when(f == pl.num_programs(1) - 1)
    def _():
        y = x_ref[...] + 0.5 * o_ref[...]
        if final:
            y = _rms_scale(y) * fn_ref[...]
        o_ref[...] = y


def _ffn(x2d, norm_w, wg, wu, wd, final_w=None, tm=1024, tf=256):
    m, d = x2d.shape
    dff = wg.shape[1]
    final = final_w is not None
    in_specs = [
        pl.BlockSpec((tm, d), lambda i, j: (i, 0)),
        pl.BlockSpec((1, d), lambda i, j: (0, 0)),
        pl.BlockSpec((d, tf), lambda i, j: (0, j)),
        pl.BlockSpec((d, tf), lambda i, j: (0, j)),
        pl.BlockSpec((tf, d), lambda i, j: (j, 0)),
    ]
    args = [x2d, norm_w.reshape(1, d), wg, wu, wd]
    if final:
        in_specs.append(pl.BlockSpec((1, d), lambda i, j: (0, 0)))
        args.append(final_w.reshape(1, d))
    return pl.pallas_call(
        functools.partial(_ffn_kernel, final=final),
        grid=(m // tm, dff // tf),
        in_specs=in_specs,
        out_specs=pl.BlockSpec((tm, d), lambda i, j: (i, 0)),
        out_shape=jax.ShapeDtypeStruct((m, d), F32),
        scratch_shapes=[pltpu.VMEM((tm, d), BF16)],
        compiler_params=_cparams(("parallel", "arbitrary"), 58),
        name="ffn",
    )(*args)


def _norm_matmul_kernel(x_ref, nw_ref, w_ref, o_ref, h_scr):
    @pl.when(pl.program_id(1) == 0)
    def _():
        h_scr[...] = (_rms_scale(x_ref[...]) * nw_ref[...]).astype(BF16)

    o_ref[...] = jnp.dot(h_scr[...], w_ref[...], preferred_element_type=F32)


def _norm_matmul(x2d, norm_w, w, tm=1024, tn=512):
    m, d = x2d.shape
    n = w.shape[1]
    return pl.pallas_call(
        _norm_matmul_kernel,
        grid=(m // tm, n // tn),
        in_specs=[
            pl.BlockSpec((tm, d), lambda i, j: (i, 0)),
            pl.BlockSpec((1, d), lambda i, j: (0, 0)),
            pl.BlockSpec((d, tn), lambda i, j: (0, j)),
        ],
        out_specs=pl.BlockSpec((tm, tn), lambda i, j: (i, j)),
        out_shape=jax.ShapeDtypeStruct((m, n), F32),
        scratch_shapes=[pltpu.VMEM((tm, d), BF16)],
        compiler_params=_cparams(("parallel", "arbitrary"), 40),
        name="norm_matmul",
    )(x2d, norm_w.reshape(1, d), w)


def _matmul_res_kernel(a_ref, w_ref, r_ref, o_ref):
    o_ref[...] = r_ref[...] + jnp.dot(a_ref[...], w_ref[...], preferred_element_type=F32)


def _matmul_res(a, w, res, tm=1024, tn=1024):
    m, k = a.shape
    n = w.shape[1]
    return pl.pallas_call(
        _matmul_res_kernel,
        grid=(m // tm, n // tn),
        in_specs=[
            pl.BlockSpec((tm, k), lambda i, j: (i, 0)),
            pl.BlockSpec((k, tn), lambda i, j: (0, j)),
            pl.BlockSpec((tm, tn), lambda i, j: (i, j)),
        ],
        out_specs=pl.BlockSpec((tm, tn), lambda i, j: (i, j)),
        out_shape=jax.ShapeDtypeStruct((m, n), F32),
        compiler_params=_cparams(("parallel", "arbitrary"), 48),
        name="matmul_res",
    )(a, w, res)


CONV_HALO = 32
SUBLANES = 8


def _layernorm(x, g, b):
    mu = jnp.mean(x, axis=-1, keepdims=True)
    xc = x - mu
    var = jnp.mean(xc * xc, axis=-1, keepdims=True)
    return xc * lax.rsqrt(var + EPS) * g + b


def _l0_mid_kernel(av_ref, ag_ref, bu_ref, bv_ref, cw_ref, cb_ref, clg_ref, clb_ref,
                   glg_ref, glb_ref, ws_ref, bst_ref, o_ref, abuf, shifted, *, ts):
    s = pl.program_id(1)

    @pl.when(s == 0)
    def _():
        abuf[0:CONV_HALO, :] = jnp.zeros((CONV_HALO, CONV_CH), F32)

    abuf[CONV_HALO:CONV_HALO + ts, :] = av_ref[0] * jax.nn.sigmoid(ag_ref[0])
    span = CONV_HALO + ts - SUBLANES
    for r in range(1, SUBLANES):
        shifted[r - 1] = abuf[r:r + span, :]
    base = CONV_HALO - (CONV_WIDTH - 1)
    acc = jnp.zeros((ts, CONV_CH), F32)
    for j in range(CONV_WIDTH):
        q, r = divmod(base + j, SUBLANES)
        src = abuf if r == 0 else shifted.at[r - 1]
        rows = src[SUBLANES * q:SUBLANES * q + ts, :]
        acc = acc + rows * cw_ref[j:j + 1, :]
    acc = acc + cb_ref[...]
    a = _layernorm(acc, clg_ref[...], clb_ref[...])
    o_ref[0, :, 0:CONV_CH] = (a * jax.nn.sigmoid(a)).astype(o_ref.dtype)
    abuf[0:CONV_HALO, :] = abuf[ts:ts + CONV_HALO, :]

    row = lax.broadcasted_iota(jnp.int32, (GMLP_CHUNK, GMLP_CHUNK), 0)
    col = lax.broadcasted_iota(jnp.int32, (GMLP_CHUNK, GMLP_CHUNK), 1)
    tri = col <= row
    for g in range(GMLP_GROUPS):
        lo = g * GMLP_GDIM
        w_g = jnp.where(tri, ws_ref[g], 0.0).astype(BF16)
        bias_g = bst_ref[:, g:g + 1]
        ln_g = glg_ref[:, lo:lo + GMLP_GDIM]
        ln_b = glb_ref[:, lo:lo + GMLP_GDIM]
        for c in range(ts // GMLP_CHUNK):
            r0 = c * GMLP_CHUNK
            u = jax.nn.gelu(bu_ref[0, r0:r0 + GMLP_CHUNK, lo:lo + GMLP_GDIM])
            v = jax.nn.gelu(bv_ref[0, r0:r0 + GMLP_CHUNK, lo:lo + GMLP_GDIM])
            v = _layernorm(v, ln_g, ln_b)
            sv = jnp.dot(w_g, v.astype(BF16), preferred_element_type=F32) + bias_g
            o_ref[0, r0:r0 + GMLP_CHUNK, CONV_CH + lo:CONV_CH + lo + GMLP_GDIM] = (
                (u * sv).astype(o_ref.dtype))


def _l0_mid(z, conv_w, conv_b, conv_ln_g, conv_ln_b, gmlp_ln_g, gmlp_ln_b, gmlp_ws, gmlp_bs, ts=256):
    b, s, _ = z.shape
    vec = lambda a: a.reshape(1, -1)
    const2 = lambda shape: pl.BlockSpec(shape, lambda i, j: (0, 0))
    zspec = lambda k: pl.BlockSpec((1, ts, CONV_CH), lambda i, j, k=k: (i, j, k))
    return pl.pallas_call(
        functools.partial(_l0_mid_kernel, ts=ts),
        grid=(b, s // ts),
        in_specs=[
            zspec(0), zspec(1), zspec(2), zspec(3),
            const2((CONV_WIDTH, CONV_CH)),
            const2((1, CONV_CH)), const2((1, CONV_CH)), const2((1, CONV_CH)),
            const2((1, GMLP_CH)), const2((1, GMLP_CH)),
            pl.BlockSpec((GMLP_GROUPS, GMLP_CHUNK, GMLP_CHUNK), lambda i, j: (0, 0, 0)),
            const2((GMLP_CHUNK, GMLP_GROUPS)),
        ],
        out_specs=pl.BlockSpec((1, ts, CONV_CH + GMLP_CH), lambda i, j: (i, j, 0)),
        out_shape=jax.ShapeDtypeStruct((b, s, CONV_CH + GMLP_CH), BF16),
        scratch_shapes=[pltpu.VMEM((CONV_HALO + ts, CONV_CH), F32),
                        pltpu.VMEM((SUBLANES - 1, CONV_HALO + ts - SUBLANES, CONV_CH), F32)],
        compiler_params=_cparams(("parallel", "arbitrary"), 32),
        name="l0_mid",
    )(z, z, z, z, conv_w.reshape(CONV_WIDTH, CONV_CH), vec(conv_b), vec(conv_ln_g), vec(conv_ln_b),
      vec(gmlp_ln_g), vec(gmlp_ln_b), gmlp_ws, gmlp_bs.T)


SEL_TK = 512
WIN_TK = 128
N_HALF = 128


def _rope(x, c, a, b):
    return x * c + pltpu.roll(x, HEAD_DIM - ROT_DIM // 2, 1) * a + pltpu.roll(x, ROT_DIM // 2, 1) * b


def _compress(k_ref, pe_ref, w1_ref, w2_ref):
    half = CMP_LEN // 2
    p = jnp.zeros((N_HALF, CMP_HIDDEN), F32)
    q = jnp.zeros((N_HALF, CMP_HIDDEN), F32)
    for l in range(half):
        rows = k_ref[0, pl.ds(l, N_HALF, stride=CMP_STRIDE), :]
        p = p + jnp.dot((rows + pe_ref[l:l + 1, :]).astype(BF16), w1_ref[l],
                        preferred_element_type=F32)
        q = q + jnp.dot((rows + pe_ref[half + l:half + l + 1, :]).astype(BF16), w1_ref[half + l],
                        preferred_element_type=F32)
    hid = p + pltpu.roll(q, N_HALF - 1, 0)
    return jnp.dot(jax.nn.gelu(hid).astype(BF16), w2_ref[...], preferred_element_type=F32)


def _nsa_prep_kernel(kc_ref, vc_ref, ks_ref, vs_ref, kw_ref, vw_ref, rc_ref, ra_ref, rb_ref,
                     pek_ref, w1k_ref, w2k_ref, pev_ref, w1v_ref, w2v_ref,
                     ksr_ref, vst_ref, kwr_ref, vwt_ref, kcmp_ref, vcmpt_ref):
    c, a, b = rc_ref[0], ra_ref[0], rb_ref[0]
    ksr_ref[0, 0] = _rope(ks_ref[0], c, a, b).astype(BF16)
    kwr_ref[0, 0] = _rope(kw_ref[0], c, a, b).astype(BF16)
    s = vs_ref.shape[1]
    for t in range(s // SEL_TK):
        vst_ref[0, 0, t] = vs_ref[0, t * SEL_TK:(t + 1) * SEL_TK, :].T.astype(BF16)
    for t in range(s // WIN_TK):
        vwt_ref[0, 0, t] = vw_ref[0, t * WIN_TK:(t + 1) * WIN_TK, :].T.astype(BF16)
    kcmp_ref[0, 0] = _compress(kc_ref, pek_ref, w1k_ref, w2k_ref).astype(BF16)
    vcmpt_ref[0, 0] = _compress(vc_ref, pev_ref, w1v_ref, w2v_ref).T.astype(BF16)


def _nsa_prep(z, rope_c, rope_a, rope_b, pe_k, w1_k, w2_k, pe_v, w1_v, w2_v):
    b, s, _ = z.shape
    first = D_MODEL // HEAD_DIM
    zspec = lambda k: pl.BlockSpec((1, s, HEAD_DIM), lambda i, g, k=k: (i, 0, first + k * N_KV + g))
    tab = pl.BlockSpec((1, s, HEAD_DIM), lambda i, g: (i, 0, 0))
    c2 = lambda shape: pl.BlockSpec(shape, lambda i, g: (0, 0))
    c3 = lambda shape: pl.BlockSpec(shape, lambda i, g: (0, 0, 0))
    out4 = lambda shape: pl.BlockSpec((1, 1) + shape, lambda i, g: (i, g, 0, 0))
    out5 = lambda shape: pl.BlockSpec((1, 1) + shape, lambda i, g: (i, g, 0, 0, 0))
    w1 = lambda w: w.reshape(CMP_LEN, HEAD_DIM, CMP_HIDDEN)
    return pl.pallas_call(
        _nsa_prep_kernel,
        grid=(b, N_KV),
        in_specs=[zspec(0), zspec(1), zspec(2), zspec(3), zspec(4), zspec(5), tab, tab, tab,
                  c2((CMP_LEN, HEAD_DIM)), c3((CMP_LEN, HEAD_DIM, CMP_HIDDEN)), c2((CMP_HIDDEN, HEAD_DIM)),
                  c2((CMP_LEN, HEAD_DIM)), c3((CMP_LEN, HEAD_DIM, CMP_HIDDEN)), c2((CMP_HIDDEN, HEAD_DIM))],
        out_specs=[out4((s, HEAD_DIM)), out5((s // SEL_TK, HEAD_DIM, SEL_TK)),
                   out4((s, HEAD_DIM)), out5((s // WIN_TK, HEAD_DIM, WIN_TK)),
                   out4((N_HALF, HEAD_DIM)), out4((HEAD_DIM, N_HALF))],
        out_shape=[jax.ShapeDtypeStruct((b, N_KV, s, HEAD_DIM), BF16),
                   jax.ShapeDtypeStruct((b, N_KV, s // SEL_TK, HEAD_DIM, SEL_TK), BF16),
                   jax.ShapeDtypeStruct((b, N_KV, s, HEAD_DIM), BF16),
                   jax.ShapeDtypeStruct((b, N_KV, s // WIN_TK, HEAD_DIM, WIN_TK), BF16),
                   jax.ShapeDtypeStruct((b, N_KV, N_HALF, HEAD_DIM), BF16),
                   jax.ShapeDtypeStruct((b, N_KV, HEAD_DIM, N_HALF), BF16)],
        compiler_params=_cparams(("parallel", "arbitrary"), 40),
        name="nsa_prep",
    )(z, z, z, z, z, z, rope_c, rope_a, rope_b,
      pe_k, w1(w1_k).astype(BF16), w2_k.astype(BF16), pe_v, w1(w1_v).astype(BF16), w2_v.astype(BF16))


Q_SCALE = SCALE * 1.4426950408889634


def _scores_t(k, q):
    return lax.dot_general(k, q, (((1,), (1,)), ((), ())), preferred_element_type=F32)


def _online_attend(q, k_ref, vt_ref, exp_ref, sel, tpos, tile, n_tiles):
    rows = q.shape[0]
    m = jnp.full((1, rows), NEG, F32)
    l = jnp.zeros((1, rows), F32)
    acc = jnp.zeros((HEAD_DIM, rows), F32)
    for kt in range(n_tiles):
        visible = jnp.dot(exp_ref[kt * tile:(kt + 1) * tile, :], sel, preferred_element_type=F32) > 0.5
        if kt == n_tiles - 1:
            kpos = kt * tile + lax.broadcasted_iota(jnp.int32, visible.shape, 0)
            visible = visible & (kpos <= tpos)
        bias = jnp.where(visible, 0.0, NEG)
        s = (_scores_t(k_ref[0, 0, kt * tile:(kt + 1) * tile, :], q)
             + jnp.concatenate([bias] * (rows // bias.shape[1]), axis=1))
        m_new = jnp.maximum(m, jnp.max(s, axis=0, keepdims=True))
        alpha = jnp.exp2(m - m_new)
        p = jnp.exp2(s - m_new)
        l = alpha * l + jnp.sum(p, axis=0, keepdims=True)
        acc = alpha * acc + jnp.dot(vt_ref[0, 0, kt], p.astype(BF16), preferred_element_type=F32)
        m = m_new
    return acc / l


def _nsa_attn_kernel(q_ref, zg_ref, rc_ref, ra_ref, rb_ref, ksr_ref, vst_ref, kwr_ref, vwt_ref,
                     kcmp_ref, vcmpt_ref, ovt_ref, exp_ref, o_ref, gate_scr, osel_scr, *, tq, seq):
    grp = pl.program_id(1)
    t0 = pl.program_id(2) * tq
    rows = HPG * tq
    heads = lambda x: jnp.concatenate([x] * HPG, axis=1)

    c, a, b = rc_ref[0], ra_ref[0], rb_ref[0]
    q_plain, q_rot = [], []
    for h in range(HPG):
        qh = q_ref[0, :, h * HEAD_DIM:(h + 1) * HEAD_DIM] * Q_SCALE
        q_plain.append(qh.astype(BF16))
        q_rot.append(_rope(qh, c, a, b).astype(BF16))
    q_plain = jnp.concatenate(q_plain, axis=0)
    q_rot = jnp.concatenate(q_rot, axis=0)

    tpos = t0 + lax.broadcasted_iota(jnp.int32, (1, tq), 1)

    s_c = _scores_t(kcmp_ref[0, 0], q_plain)
    n_idx = lax.broadcasted_iota(jnp.int32, (N_HALF, rows), 0)
    cmask = n_idx * CMP_STRIDE + (CMP_LEN - 1) <= heads(tpos)
    s_c = jnp.where(cmask, s_c, NEG)
    e = jnp.exp2(s_c - jnp.max(s_c, axis=0, keepdims=True))
    p_c = jnp.where(cmask, e / jnp.sum(e, axis=0, keepdims=True), 0.0)
    o_c = jnp.dot(vcmpt_ref[0, 0], p_c.astype(BF16), preferred_element_type=F32)

    span = WINDOW + tq
    k0 = pl.multiple_of(jnp.maximum(t0 - WINDOW, 0), WIN_TK)
    kp = k0 + lax.broadcasted_iota(jnp.int32, (span, tq), 0)
    w_bias = jnp.where((kp <= tpos) & (kp > tpos - WINDOW), 0.0, NEG)
    s_w = _scores_t(kwr_ref[0, 0, pl.ds(k0, span), :], q_rot) + heads(w_bias)
    p_w = jnp.exp2(s_w - jnp.max(s_w, axis=0, keepdims=True))
    l_w = jnp.sum(p_w, axis=0, keepdims=True)
    p_w = p_w.astype(BF16)
    o_w = jnp.zeros((HEAD_DIM, rows), F32)
    for i in range(span // WIN_TK):
        o_w = o_w + jnp.dot(vwt_ref[0, 0, k0 // WIN_TK + i], p_w[i * WIN_TK:(i + 1) * WIN_TK, :],
                            preferred_element_type=F32)
    o_w = o_w / l_w

    p_sum = p_c[:, 0:tq]
    for h in range(1, HPG):
        p_sum = p_sum + p_c[:, h * tq:(h + 1) * tq]
    p_hi = p_sum.astype(BF16)
    p_lo = (p_sum - p_hi.astype(F32)).astype(BF16)
    ovt = ovt_ref[...]
    imp = (jnp.dot(ovt, p_hi, preferred_element_type=F32)
           + jnp.dot(ovt, p_lo, preferred_element_type=F32))
    n_blk = seq // SEL_BLOCK
    j_idx = lax.broadcasted_iota(jnp.int32, (n_blk, tq), 0)
    cur = tpos // SEL_BLOCK
    forced = (j_idx == 0) | (j_idx == cur) | (j_idx == cur - 1)
    imp = imp + jnp.where(forced, FORCE_BONUS, 0.0)
    imp = jnp.where(j_idx * SEL_BLOCK <= tpos, imp, NEG)
    rank = jnp.zeros((n_blk, tq), jnp.int32)
    for i in range(n_blk):
        vi = imp[i:i + 1, :]
        ahead = (vi > imp) | ((vi == imp) & (j_idx > i))
        rank = rank + ahead.astype(jnp.int32)
    sel = (rank < min(SEL_TOPK, n_blk)).astype(BF16)

    n_tiles =(t0 + tq + SEL_TK - 1) // SEL_TK
    for n in range(1, seq // SEL_TK + 1):
        @pl.when(n_tiles == n)
        def _(n=n):
            osel_scr[...] = _online_attend(q_rot, ksr_ref, vst_ref, exp_ref, sel, tpos, SEL_TK, n)
    o_s = osel_scr[...]

    gate_scr[...] = jax.nn.sigmoid(zg_ref[0]).T

    def gate(branch):
        return jnp.concatenate(
            [gate_scr[pl.ds(3 * (HPG * grp + h) + branch, 1), :] for h in range(HPG)], axis=1)

    o = gate(0) * o_c + gate(1) * o_s + gate(2) * o_w
    for h in range(HPG):
        o_ref[0, :, h * HEAD_DIM:(h + 1) * HEAD_DIM] = o[:, h * tq:(h + 1) * tq].T.astype(o_ref.dtype)


def _nsa_attn(z, rope_c, rope_a, rope_b, ksr, vst, kwr, vwt, kcmp, vcmpt, tq=256):
    b, s, _ = z.shape
    assert SEL_TK % tq == 0 and s % SEL_TK == 0
    n_blk = s // SEL_BLOCK
    n = np.arange(N_HALF)[None, :]
    j = np.arange(n_blk)[:, None]
    ovt = ((n < N_HALF - 1) & (n * CMP_STRIDE < (j + 1) * SEL_BLOCK)
           & (n * CMP_STRIDE + CMP_LEN - 1 >= j * SEL_BLOCK))
    expand = (np.arange(s)[:, None] // SEL_BLOCK) == np.arange(n_blk)[None, :]
    gate_blk = NSA_MAIN // HEAD_DIM
    tab = pl.BlockSpec((1, tq, HEAD_DIM), lambda i, g, t: (i, t, 0))
    kv4 = lambda shape: pl.BlockSpec((1, 1) + shape, lambda i, g, t: (i, g, 0, 0))
    kv5 = lambda shape: pl.BlockSpec((1, 1) + shape, lambda i, g, t: (i, g, 0, 0, 0))
    return pl.pallas_call(
        functools.partial(_nsa_attn_kernel, tq=tq, seq=s),
        grid=(b, N_KV, s // tq),
        in_specs=[
            pl.BlockSpec((1, tq, HPG * HEAD_DIM), lambda i, g, t: (i, t, g)),
            pl.BlockSpec((1, tq, HEAD_DIM), lambda i, g, t: (i, t, gate_blk)),
            tab, tab, tab,
            kv4((s, HEAD_DIM)), kv5((s // SEL_TK, HEAD_DIM, SEL_TK)),
            kv4((s, HEAD_DIM)), kv5((s // WIN_TK, HEAD_DIM, WIN_TK)),
            kv4((N_HALF, HEAD_DIM)), kv4((HEAD_DIM, N_HALF)),
            pl.BlockSpec((n_blk, N_HALF), lambda i, g, t: (0, 0)),
            pl.BlockSpec((s, n_blk), lambda i, g, t: (0, 0)),
        ],
        out_specs=pl.BlockSpec((1, tq, HPG * HEAD_DIM), lambda i, g, t: (i, t, g)),
        out_shape=jax.ShapeDtypeStruct((b, s, D_MODEL), BF16),
        scratch_shapes=[pltpu.VMEM((HEAD_DIM, tq), F32), pltpu.VMEM((HEAD_DIM, HPG * tq), F32)],
        compiler_params=_cparams(("parallel", "parallel", "arbitrary"), 40),
        name="nsa_attn",
    )(z, z, rope_c, rope_a, rope_b, ksr, vst, kwr, vwt, kcmp, vcmpt,
      jnp.asarray(ovt, BF16), jnp.asarray(expand, BF16))


def _rope_tables(positions):
    half = ROT_DIM // 2
    inv = ROPE_THETA ** (-jnp.arange(0, ROT_DIM, 2, dtype=F32) / ROT_DIM)
    ang = positions.astype(F32)[..., None] * inv
    cos, sin = jnp.cos(ang), jnp.sin(ang)
    zeros = lambda w: jnp.zeros(cos.shape[:-1] + (w,), F32)
    c = jnp.concatenate([cos, cos, jnp.ones(cos.shape[:-1] + (HEAD_DIM - ROT_DIM,), F32)], axis=-1)
    a = jnp.concatenate([-sin, zeros(HEAD_DIM - half)], axis=-1)
    b = jnp.concatenate([zeros(half), sin, zeros(HEAD_DIM - ROT_DIM)], axis=-1)
    return c, a, b


def kernel(x, positions, l0_ffn1_norm, l0_ffn1_w_gate, l0_ffn1_w_up, l0_ffn1_w_down, l0_mix_norm, l0_w_in, l0_conv_w, l0_conv_b, l0_conv_ln_g, l0_conv_ln_b, l0_gmlp_ln_g, l0_gmlp_ln_b, l0_gmlp_ws, l0_gmlp_bs, l0_w_out, l0_ffn2_norm, l0_ffn2_w_gate, l0_ffn2_w_up, l0_ffn2_w_down, l1_ffn1_norm, l1_ffn1_w_gate, l1_ffn1_w_up, l1_ffn1_w_down, l1_mix_norm, l1_w_in, l1_cmp_pe_k, l1_cmp_w1_k, l1_cmp_w2_k, l1_cmp_pe_v, l1_cmp_w1_v, l1_cmp_w2_v, l1_w_out, l1_ffn2_norm, l1_ffn2_w_gate, l1_ffn2_w_up, l1_ffn2_w_down, final_norm):
    b, s, d = x.shape
    m = b * s
    bf = lambda w: w.astype(BF16)
    x2 = x.reshape(m, d)

    x2 = _ffn(x2, l0_ffn1_norm, l0_ffn1_w_gate, l0_ffn1_w_up, l0_ffn1_w_down)
    z = _norm_matmul(x2, l0_mix_norm, bf(l0_w_in)).reshape(b, s, -1)
    cat = _l0_mid(z, l0_conv_w, l0_conv_b, l0_conv_ln_g, l0_conv_ln_b,
                  l0_gmlp_ln_g, l0_gmlp_ln_b, l0_gmlp_ws, l0_gmlp_bs)
    x2 = _matmul_res(cat.reshape(m, -1), bf(l0_w_out), x2)
    x2 = _ffn(x2, l0_ffn2_norm, l0_ffn2_w_gate, l0_ffn2_w_up, l0_ffn2_w_down)

    x2 = _ffn(x2, l1_ffn1_norm, l1_ffn1_w_gate, l1_ffn1_w_up, l1_ffn1_w_down)
    w_in = jnp.pad(bf(l1_w_in), ((0, 0), (0, NSA_PAD - NSA_MAIN - NSA_GATES)))
    z = _norm_matmul(x2, l1_mix_norm, w_in).reshape(b, s, -1)
    rope_c, rope_a, rope_b = _rope_tables(positions)
    ksr, vst, kwr, vwt, kcmp, vcmpt = _nsa_prep(
        z, rope_c, rope_a, rope_b, l1_cmp_pe_k, l1_cmp_w1_k, l1_cmp_w2_k,
        l1_cmp_pe_v, l1_cmp_w1_v, l1_cmp_w2_v)
    o = _nsa_attn(z, rope_c, rope_a, rope_b, ksr, vst, kwr, vwt, kcmp, vcmpt)
    x2 = _matmul_res(o.reshape(m, -1), bf(l1_w_out), x2)
    x2 = _ffn(x2, l1_ffn2_norm, l1_ffn2_w_gate, l1_ffn2_w_up, l1_ffn2_w_down,
              final_w=final_norm)
    return x2.reshape(b, s, d)
```

```python
import functools

import numpy as np
import jax
import jax.numpy as jnp
from jax import lax
from jax.experimental import pallas as pl
from jax.experimental.pallas import tpu as pltpu

F32 = jnp.float32
BF16 = jnp.bfloat16

D_MODEL = 2048
D_FF = 5632
EPS = 1e-6
CONV_CH = 1024
CONV_WIDTH = 31
GMLP_CH = 1024
GMLP_GROUPS = 8
GMLP_GDIM = 128
GMLP_CHUNK = 128
HEAD_DIM = 128
N_HEADS = 16
N_KV = 4
HPG = 4
KV_W = N_KV * HEAD_DIM
ROT_DIM = 32
ROPE_THETA = 500000.0
CMP_LEN = 32
CMP_STRIDE = 16
CMP_HIDDEN = 256
SEL_BLOCK = 64
SEL_TOPK = 16
WINDOW = 512
FORCE_BONUS = 1e3
NEG = -1e30
NSA_MAIN = D_MODEL + 6 * KV_W
NSA_GATES = 3 * N_HEADS
NSA_PAD = 5632
SCALE = HEAD_DIM ** -0.5

MIB = 1024 * 1024


def _cparams(semantics, vmem_mib):
    return pltpu.CompilerParams(dimension_semantics=semantics,
                                vmem_limit_bytes=vmem_mib * MIB)


def _rms_scale(x):
    return x * lax.rsqrt(jnp.mean(x * x, axis=-1, keepdims=True) + EPS)


DOWN_TN = 512


def _ffn_kernel(x_ref, nw_ref, wg_ref, wu_ref, wd_ref, *rest, final):
    if final:
        fn_ref, o_ref, h_scr = rest
    else:
        o_ref, h_scr = rest
    f = pl.program_id(1)

    @pl.when(f == 0)
    def _():
        h_scr[...] = (_rms_scale(x_ref[...]) * nw_ref[...]).astype(BF16)
        o_ref[...] = jnp.zeros(o_ref.shape, F32)

    h = h_scr[...]
    g = jnp.dot(h, wg_ref[...].astype(BF16), preferred_element_type=F32)
    u = jnp.dot(h, wu_ref[...].astype(BF16), preferred_element_type=F32)
    a = (g * jax.nn.sigmoid(g) * u).astype(BF16)
    for c in range(0, o_ref.shape[1], DOWN_TN):
        o_ref[:, c:c + DOWN_TN] += jnp.dot(a, wd_ref[:, c:c + DOWN_TN].astype(BF16),
                                           preferred_element_type=F32)

    @pl.when(f == pl.num_programs(1) - 1)
    def _():
        y = x_ref[...] + 0.5 * o_ref[...]
        if final:
            y = _rms_scale(y) * fn_ref[...]
        o_ref[...] = y


def _ffn(x2d, norm_w, wg, wu, wd, final_w=None, tm=1024, tf=256):
    m, d = x2d.shape
    dff = wg.shape[1]
    final = final_w is not None
    in_specs = [
        pl.BlockSpec((tm, d), lambda i, j: (i, 0)),
        pl.BlockSpec((1, d), lambda i, j: (0, 0)),
        pl.BlockSpec((d, tf), lambda i, j: (0, j)),
        pl.BlockSpec((d, tf), lambda i, j: (0, j)),
        pl.BlockSpec((tf, d), lambda i, j: (j, 0)),
    ]
    args = [x2d, norm_w.reshape(1, d), wg, wu, wd]
    if final:
        in_specs.append(pl.BlockSpec((1, d), lambda i, j: (0, 0)))
        args.append(final_w.reshape(1, d))
    return pl.pallas_call(
        functools.partial(_ffn_kernel, final=final),
        grid=(m // tm, dff // tf),
        in_specs=in_specs,
        out_specs=pl.BlockSpec((tm, d), lambda i, j: (i, 0)),
        out_shape=jax.ShapeDtypeStruct((m, d), F32),
        scratch_shapes=[pltpu.VMEM((tm, d), BF16)],
        compiler_params=_cparams(("parallel", "arbitrary"), 58),
        name="ffn",
    )(*args)


def _norm_matmul_kernel(x_ref, nw_ref, w_ref, o_ref, h_scr):
    @pl.when(pl.program_id(1) == 0)
    def _():
        h_scr[...] = (_rms_scale(x_ref[...]) * nw_ref[...]).astype(BF16)

    o_ref[...] = jnp.dot(h_scr[...], w_ref[...], preferred_element_type=F32)


def _norm_matmul(x2d, norm_w, w, tm=1024, n_col_tiles=4):
    m, d = x2d.shape
    n = w.shape[1]
    tn = n // n_col_tiles
    assert tn * n_col_tiles == n and tn % HEAD_DIM == 0
    return pl.pallas_call(
        _norm_matmul_kernel,
        grid=(m // tm, n // tn),
        in_specs=[
            pl.BlockSpec((tm, d), lambda i, j: (i, 0)),
            pl.BlockSpec((1, d), lambda i, j: (0, 0)),
            pl.BlockSpec((d, tn), lambda i, j: (0, j)),
        ],
        out_specs=pl.BlockSpec((tm, tn), lambda i, j: (i, j)),
        out_shape=jax.ShapeDtypeStruct((m, n), F32),
        scratch_shapes=[pltpu.VMEM((tm, d), BF16)],
        compiler_params=_cparams(("parallel", "arbitrary"), 56),
        name="norm_matmul",
    )(x2d, norm_w.reshape(1, d), w)


def _matmul_res_kernel(a_ref, w_ref, r_ref, o_ref):
    o_ref[...] = r_ref[...] + jnp.dot(a_ref[...], w_ref[...], preferred_element_type=F32)


def _matmul_res(a, w, res, tm=512, tn=1024):
    m, k = a.shape
    n = w.shape[1]
    return pl.pallas_call(
        _matmul_res_kernel,
        grid=(m // tm, n // tn),
        in_specs=[
            pl.BlockSpec((tm, k), lambda i, j: (i, 0)),
            pl.BlockSpec((k, tn), lambda i, j: (0, j)),
            pl.BlockSpec((tm, tn), lambda i, j: (i, j)),
        ],
        out_specs=pl.BlockSpec((tm, tn), lambda i, j: (i, j)),
        out_shape=jax.ShapeDtypeStruct((m, n), F32),
        compiler_params=_cparams(("parallel", "arbitrary"), 48),
        name="matmul_res",
    )(a, w, res)


CONV_HALO = 32
SUBLANES = 8


def _layernorm(x, g, b):
    mu = jnp.mean(x, axis=-1, keepdims=True)
    xc = x - mu
    var = jnp.mean(xc * xc, axis=-1, keepdims=True)
    return xc * lax.rsqrt(var + EPS) * g + b


def _l0_mid_kernel(av_ref, ag_ref, bu_ref, bv_ref, cw_ref, cb_ref, clg_ref, clb_ref,
                   glg_ref, glb_ref, ws_ref, bst_ref, o_ref, abuf, shifted, *, ts):
    s = pl.program_id(1)

    @pl.when(s == 0)
    def _():
        abuf[0:CONV_HALO, :] = jnp.zeros((CONV_HALO, CONV_CH), F32)

    abuf[CONV_HALO:CONV_HALO + ts, :] = av_ref[0] * jax.nn.sigmoid(ag_ref[0])
    span = CONV_HALO + ts - SUBLANES
    for r in range(1, SUBLANES):
        shifted[r - 1] = abuf[r:r + span, :]
    base = CONV_HALO - (CONV_WIDTH - 1)
    acc = jnp.zeros((ts, CONV_CH), F32)
    for j in range(CONV_WIDTH):
        q, r = divmod(base + j, SUBLANES)
        src = abuf if r == 0 else shifted.at[r - 1]
        rows = src[SUBLANES * q:SUBLANES * q + ts, :]
        acc = acc + rows * cw_ref[j:j + 1, :]
    acc = acc + cb_ref[...]
    a = _layernorm(acc, clg_ref[...], clb_ref[...])
    o_ref[0, :, 0:CONV_CH] = (a * jax.nn.sigmoid(a)).astype(o_ref.dtype)
    abuf[0:CONV_HALO, :] = abuf[ts:ts + CONV_HALO, :]

    row = lax.broadcasted_iota(jnp.int32, (GMLP_CHUNK, GMLP_CHUNK), 0)
    col = lax.broadcasted_iota(jnp.int32, (GMLP_CHUNK, GMLP_CHUNK), 1)
    tri = col <= row
    for g in range(GMLP_GROUPS):
        lo = g * GMLP_GDIM
        w_g = jnp.where(tri, ws_ref[g], 0.0).astype(BF16)
        bias_g = bst_ref[:, g:g + 1]
        ln_g = glg_ref[:, lo:lo + GMLP_GDIM]
        ln_b = glb_ref[:, lo:lo + GMLP_GDIM]
        for c in range(ts // GMLP_CHUNK):
            r0 = c * GMLP_CHUNK
            u = jax.nn.gelu(bu_ref[0, r0:r0 + GMLP_CHUNK, lo:lo + GMLP_GDIM])
            v = jax.nn.gelu(bv_ref[0, r0:r0 + GMLP_CHUNK, lo:lo + GMLP_GDIM])
            v = _layernorm(v, ln_g, ln_b)
            sv = jnp.dot(w_g, v.astype(BF16), preferred_element_type=F32) + bias_g
            o_ref[0, r0:r0 + GMLP_CHUNK, CONV_CH + lo:CONV_CH + lo + GMLP_GDIM] = (
                (u * sv).astype(o_ref.dtype))


def _l0_mid(z, conv_w, conv_b, conv_ln_g, conv_ln_b, gmlp_ln_g, gmlp_ln_b, gmlp_ws, gmlp_bs, ts=256):
    b, s, _ = z.shape
    vec = lambda a: a.reshape(1, -1)
    const2 = lambda shape: pl.BlockSpec(shape, lambda i, j: (0, 0))
    zspec = lambda k: pl.BlockSpec((1, ts, CONV_CH), lambda i, j, k=k: (i, j, k))
    return pl.pallas_call(
        functools.partial(_l0_mid_kernel, ts=ts),
        grid=(b, s // ts),
        in_specs=[
            zspec(0), zspec(1), zspec(2), zspec(3),
            const2((CONV_WIDTH, CONV_CH)),
            const2((1, CONV_CH)), const2((1, CONV_CH)), const2((1, CONV_CH)),
            const2((1, GMLP_CH)), const2((1, GMLP_CH)),
            pl.BlockSpec((GMLP_GROUPS, GMLP_CHUNK, GMLP_CHUNK), lambda i, j: (0, 0, 0)),
            const2((GMLP_CHUNK, GMLP_GROUPS)),
        ],
        out_specs=pl.BlockSpec((1, ts, CONV_CH + GMLP_CH), lambda i, j: (i, j, 0)),
        out_shape=jax.ShapeDtypeStruct((b, s, CONV_CH + GMLP_CH), BF16),
        scratch_shapes=[pltpu.VMEM((CONV_HALO + ts, CONV_CH), F32),
                        pltpu.VMEM((SUBLANES - 1, CONV_HALO + ts - SUBLANES, CONV_CH), F32)],
        compiler_params=_cparams(("parallel", "arbitrary"), 32),
        name="l0_mid",
    )(z, z, z, z, conv_w.reshape(CONV_WIDTH, CONV_CH), vec(conv_b), vec(conv_ln_g), vec(conv_ln_b),
      vec(gmlp_ln_g), vec(gmlp_ln_b), gmlp_ws, gmlp_bs.T)


SEL_TK = 512
WIN_TK = 128
N_HALF = 128


def _rope(x, c, a, b):
    return x * c + pltpu.roll(x, HEAD_DIM - ROT_DIM // 2, 1) * a + pltpu.roll(x, ROT_DIM // 2, 1) * b


def _compress(k_ref, pe_ref, w1_ref, w2_ref):
    half = CMP_LEN // 2
    p = jnp.zeros((N_HALF, CMP_HIDDEN), F32)
    q = jnp.zeros((N_HALF, CMP_HIDDEN), F32)
    for l in range(half):
        rows = k_ref[0, pl.ds(l, N_HALF, stride=CMP_STRIDE), :]
        p = p + jnp.dot((rows + pe_ref[l:l + 1, :]).astype(BF16), w1_ref[l],
                        preferred_element_type=F32)
        q = q + jnp.dot((rows + pe_ref[half + l:half + l + 1, :]).astype(BF16), w1_ref[half + l],
                        preferred_element_type=F32)
    hid = p + pltpu.roll(q, N_HALF - 1, 0)
    return jnp.dot(jax.nn.gelu(hid).astype(BF16), w2_ref[...], preferred_element_type=F32)


def _nsa_prep_kernel(kc_ref, vc_ref, ks_ref, vs_ref, kw_ref, vw_ref, rc_ref, ra_ref, rb_ref,
                     pek_ref, w1k_ref, w2k_ref, pev_ref, w1v_ref, w2v_ref,
                     ksr_ref, vst_ref, kwr_ref, vwt_ref, kcmp_ref, vcmpt_ref):
    c, a, b = rc_ref[0], ra_ref[0], rb_ref[0]
    ksr_ref[0, 0] = _rope(ks_ref[0], c, a, b).astype(BF16)
    kwr_ref[0, 0] = _rope(kw_ref[0], c, a, b).astype(BF16)
    s = vs_ref.shape[1]
    for t in range(s // SEL_TK):
        vst_ref[0, 0, t] = vs_ref[0, t * SEL_TK:(t + 1) * SEL_TK, :].T.astype(BF16)
    for t in range(s // WIN_TK):
        vwt_ref[0, 0, t] = vw_ref[0, t * WIN_TK:(t + 1) * WIN_TK, :].T.astype(BF16)
    kcmp_ref[0, 0] = _compress(kc_ref, pek_ref, w1k_ref, w2k_ref).astype(BF16)
    vcmpt_ref[0, 0] = _compress(vc_ref, pev_ref, w1v_ref, w2v_ref).T.astype(BF16)


def _nsa_prep(z, rope_c, rope_a, rope_b, pe_k, w1_k, w2_k, pe_v, w1_v, w2_v):
    b, s, _ = z.shape
    first = D_MODEL // HEAD_DIM
    zspec = lambda k: pl.BlockSpec((1, s, HEAD_DIM), lambda i, g, k=k: (i, 0, first + k * N_KV + g))
    tab = pl.BlockSpec((1, s, HEAD_DIM), lambda i, g: (i, 0, 0))
    c2 = lambda shape: pl.BlockSpec(shape, lambda i, g: (0, 0))
    c3 = lambda shape: pl.BlockSpec(shape, lambda i, g: (0, 0, 0))
    out4 = lambda shape: pl.BlockSpec((1, 1) + shape, lambda i, g: (i, g, 0, 0))
    out5 = lambda shape: pl.BlockSpec((1, 1) + shape, lambda i, g: (i, g, 0, 0, 0))
    w1 = lambda w: w.reshape(CMP_LEN, HEAD_DIM, CMP_HIDDEN)
    return pl.pallas_call(
        _nsa_prep_kernel,
        grid=(b, N_KV),
        in_specs=[zspec(0), zspec(1), zspec(2), zspec(3), zspec(4), zspec(5), tab, tab, tab,
                  c2((CMP_LEN, HEAD_DIM)), c3((CMP_LEN, HEAD_DIM, CMP_HIDDEN)), c2((CMP_HIDDEN, HEAD_DIM)),
                  c2((CMP_LEN, HEAD_DIM)), c3((CMP_LEN, HEAD_DIM, CMP_HIDDEN)), c2((CMP_HIDDEN, HEAD_DIM))],
        out_specs=[out4((s, HEAD_DIM)), out5((s // SEL_TK, HEAD_DIM, SEL_TK)),
                   out4((s, HEAD_DIM)), out5((s // WIN_TK, HEAD_DIM, WIN_TK)),
                   out4((N_HALF, HEAD_DIM)), out4((HEAD_DIM, N_HALF))],
        out_shape=[jax.ShapeDtypeStruct((b, N_KV, s, HEAD_DIM), BF16),
                   jax.ShapeDtypeStruct((b, N_KV, s // SEL_TK, HEAD_DIM, SEL_TK), BF16),
                   jax.ShapeDtypeStruct((b, N_KV, s, HEAD_DIM), BF16),
                   jax.ShapeDtypeStruct((b, N_KV, s // WIN_TK, HEAD_DIM, WIN_TK), BF16),
                   jax.ShapeDtypeStruct((b, N_KV, N_HALF, HEAD_DIM), BF16),
                   jax.ShapeDtypeStruct((b, N_KV, HEAD_DIM, N_HALF), BF16)],
        compiler_params=_cparams(("parallel", "arbitrary"), 40),
        name="nsa_prep",
    )(z, z, z, z, z, z, rope_c, rope_a, rope_b,
      pe_k, w1(w1_k).astype(BF16), w2_k.astype(BF16), pe_v, w1(w1_v).astype(BF16), w2_v.astype(BF16))


Q_SCALE = SCALE * 1.4426950408889634


def _scores_t(k, q):
    return lax.dot_general(k, q, (((1,), (1,)), ((), ())), preferred_element_type=F32)


def _online_attend(q, k_ref, vt_ref, sel, tpos, tile, n_tiles):
    rows = q.shape[0]
    m = jnp.full((1, rows), NEG, F32)
    l = jnp.zeros((1, rows), F32)
    acc = jnp.zeros((HEAD_DIM, rows), F32)
    for kt in range(n_tiles):
        blocks = tile // SEL_BLOCK
        sel_rows = sel[kt * blocks:(kt + 1) * blocks, :]
        visible = jnp.broadcast_to(sel_rows[:, None, :], (blocks, SEL_BLOCK, sel.shape[1])
                                   ).reshape(tile, sel.shape[1]) > 0.5
        if kt == n_tiles - 1:
            kpos = kt * tile + lax.broadcasted_iota(jnp.int32, visible.shape, 0)
            visible = visible & (kpos <= tpos)
        bias = jnp.where(visible, 0.0, NEG)
        s = (_scores_t(k_ref[0, 0, kt * tile:(kt + 1) * tile, :], q)
             + jnp.concatenate([bias] * (rows // bias.shape[1]), axis=1))
        m_new = jnp.maximum(m, jnp.max(s, axis=0, keepdims=True))
        alpha = jnp.exp2(m - m_new)
        p = jnp.exp2(s - m_new)
        l = alpha * l + jnp.sum(p, axis=0, keepdims=True)
        acc = alpha * acc + jnp.dot(vt_ref[0, 0, kt], p.astype(BF16), preferred_element_type=F32)
        m = m_new
    return acc / l


def _nsa_attn_kernel(q_ref, zg_ref, rc_ref, ra_ref, rb_ref, ksr_ref, vst_ref, kwr_ref, vwt_ref,
                     kcmp_ref, vcmpt_ref, ovt_ref, o_ref, gate_scr, osel_scr, *, tq, seq):
    grp = pl.program_id(1)
    t0 = pl.program_id(2) * tq
    rows = HPG * tq
    heads = lambda x: jnp.concatenate([x] * HPG, axis=1)

    c, a, b = rc_ref[0], ra_ref[0], rb_ref[0]
    q_plain, q_rot = [], []
    for h in range(HPG):
        qh = q_ref[0, :, h * HEAD_DIM:(h + 1) * HEAD_DIM] * Q_SCALE
        q_plain.append(qh.astype(BF16))
        q_rot.append(_rope(qh, c, a, b).astype(BF16))
    q_plain = jnp.concatenate(q_plain, axis=0)
    q_rot = jnp.concatenate(q_rot, axis=0)

    tpos = t0 + lax.broadcasted_iota(jnp.int32, (1, tq), 1)

    s_c = _scores_t(kcmp_ref[0, 0], q_plain)
    n_idx = lax.broadcasted_iota(jnp.int32, (N_HALF, rows), 0)
    cmask = n_idx * CMP_STRIDE + (CMP_LEN - 1) <= heads(tpos)
    s_c = jnp.where(cmask, s_c, NEG)
    e = jnp.exp2(s_c - jnp.max(s_c, axis=0, keepdims=True))
    p_c = jnp.where(cmask, e / jnp.sum(e, axis=0, keepdims=True), 0.0)
    o_c = jnp.dot(vcmpt_ref[0, 0], p_c.astype(BF16), preferred_element_type=F32)

    span = WINDOW + tq
    k0 = pl.multiple_of(jnp.maximum(t0 - WINDOW, 0), WIN_TK)
    kp = k0 + lax.broadcasted_iota(jnp.int32, (span, tq), 0)
    w_bias = jnp.where((kp <= tpos) & (kp > tpos - WINDOW), 0.0, NEG)
    s_w = _scores_t(kwr_ref[0, 0, pl.ds(k0, span), :], q_rot) + heads(w_bias)
    p_w = jnp.exp2(s_w - jnp.max(s_w, axis=0, keepdims=True))
    l_w = jnp.sum(p_w, axis=0, keepdims=True)
    p_w = p_w.astype(BF16)
    o_w = jnp.zeros((HEAD_DIM, rows), F32)
    for i in range(span // WIN_TK):
        o_w = o_w + jnp.dot(vwt_ref[0, 0, k0 // WIN_TK + i], p_w[i * WIN_TK:(i + 1) * WIN_TK, :],
                            preferred_element_type=F32)
    o_w = o_w / l_w

    p_sum = p_c[:, 0:tq]
    for h in range(1, HPG):
        p_sum = p_sum + p_c[:, h * tq:(h + 1) * tq]
    p_hi = p_sum.astype(BF16)
    p_lo = (p_sum - p_hi.astype(F32)).astype(BF16)
    ovt = ovt_ref[...]
    imp = (jnp.dot(ovt, p_hi, preferred_element_type=F32)
           + jnp.dot(ovt, p_lo, preferred_element_type=F32))
    n_blk = seq // SEL_BLOCK
    j_idx = lax.broadcasted_iota(jnp.int32, (n_blk, tq), 0)
    cur = tpos // SEL_BLOCK
    forced = (j_idx == 0) | (j_idx == cur) | (j_idx == cur - 1)
    imp = imp + jnp.where(forced, FORCE_BONUS, 0.0)
    imp = jnp.where(j_idx * SEL_BLOCK <= tpos, imp, NEG)
    rank = jnp.zeros((n_blk, tq), jnp.int32)
    for i in range(n_blk):
        vi = imp[i:i + 1, :]
        ahead = (vi > imp) | ((vi == imp) & (j_idx > i))
        rank = rank + ahead.astype(jnp.int32)
    sel = (rank < min(SEL_TOPK, n_blk)).astype(F32)

    n_tiles =(t0 + tq + SEL_TK - 1) // SEL_TK
    for n in range(1, seq // SEL_TK + 1):
        @pl.when(n_tiles == n)
        def _(n=n):
            osel_scr[...] = _online_attend(q_rot, ksr_ref, vst_ref, sel, tpos, SEL_TK, n)
    o_s = osel_scr[...]

    gate_scr[...] = jax.nn.sigmoid(zg_ref[0]).T

    def gate(branch):
        return jnp.concatenate(
            [gate_scr[pl.ds(3 * (HPG * grp + h) + branch, 1), :] for h in range(HPG)], axis=1)

    o = gate(0) * o_c + gate(1) * o_s + gate(2) * o_w
    for h in range(HPG):
        o_ref[0, :, h * HEAD_DIM:(h + 1) * HEAD_DIM] = o[:, h * tq:(h + 1) * tq].T.astype(o_ref.dtype)


def _nsa_attn(z, rope_c, rope_a, rope_b, ksr, vst, kwr, vwt, kcmp, vcmpt, tq=256):
    b, s, _ = z.shape
    assert SEL_TK % tq == 0 and s % SEL_TK == 0
    n_blk = s // SEL_BLOCK
    n = np.arange(N_HALF)[None, :]
    j = np.arange(n_blk)[:, None]
    ovt = ((n < N_HALF - 1) & (n * CMP_STRIDE < (j + 1) * SEL_BLOCK)
           & (n * CMP_STRIDE + CMP_LEN - 1 >= j * SEL_BLOCK))
    gate_blk = NSA_MAIN // HEAD_DIM
    tab = pl.BlockSpec((1, tq, HEAD_DIM), lambda i, g, t: (i, t, 0))
    kv4 = lambda shape: pl.BlockSpec((1, 1) + shape, lambda i, g, t: (i, g, 0, 0))
    kv5 = lambda shape: pl.BlockSpec((1, 1) + shape, lambda i, g, t: (i, g, 0, 0, 0))
    return pl.pallas_call(
        functools.partial(_nsa_attn_kernel, tq=tq, seq=s),
        grid=(b, N_KV, s // tq),
        in_specs=[
            pl.BlockSpec((1, tq, HPG * HEAD_DIM), lambda i, g, t: (i, t, g)),
            pl.BlockSpec((1, tq, HEAD_DIM), lambda i, g, t: (i, t, gate_blk)),
            tab, tab, tab,
            kv4((s, HEAD_DIM)), kv5((s // SEL_TK, HEAD_DIM, SEL_TK)),
            kv4((s, HEAD_DIM)), kv5((s // WIN_TK, HEAD_DIM, WIN_TK)),
            kv4((N_HALF, HEAD_DIM)), kv4((HEAD_DIM, N_HALF)),
            pl.BlockSpec((n_blk, N_HALF), lambda i, g, t: (0, 0)),
        ],
        out_specs=pl.BlockSpec((1, tq, HPG * HEAD_DIM), lambda i, g, t: (i, t, g)),
        out_shape=jax.ShapeDtypeStruct((b, s, D_MODEL), BF16),
        scratch_shapes=[pltpu.VMEM((HEAD_DIM, tq), F32), pltpu.VMEM((HEAD_DIM, HPG * tq), F32)],
        compiler_params=_cparams(("parallel", "parallel", "arbitrary"), 40),
        name="nsa_attn",
    )(z, z, rope_c, rope_a, rope_b, ksr, vst, kwr, vwt, kcmp, vcmpt,
      jnp.asarray(ovt, BF16))


def _rope_tables(positions):
    half = ROT_DIM // 2
    inv = ROPE_THETA ** (-jnp.arange(0, ROT_DIM, 2, dtype=F32) / ROT_DIM)
    ang = positions.astype(F32)[..., None] * inv
    cos, sin = jnp.cos(ang), jnp.sin(ang)
    zeros = lambda w: jnp.zeros(cos.shape[:-1] + (w,), F32)
    c = jnp.concatenate([cos, cos, jnp.ones(cos.shape[:-1] + (HEAD_DIM - ROT_DIM,), F32)], axis=-1)
    a = jnp.concatenate([-sin, zeros(HEAD_DIM - half)], axis=-1)
    b = jnp.concatenate([zeros(half), sin, zeros(HEAD_DIM - ROT_DIM)], axis=-1)
    return c, a, b


def kernel(x, positions, l0_ffn1_norm, l0_ffn1_w_gate, l0_ffn1_w_up, l0_ffn1_w_down, l0_mix_norm, l0_w_in, l0_conv_w, l0_conv_b, l0_conv_ln_g, l0_conv_ln_b, l0_gmlp_ln_g, l0_gmlp_ln_b, l0_gmlp_ws, l0_gmlp_bs, l0_w_out, l0_ffn2_norm, l0_ffn2_w_gate, l0_ffn2_w_up, l0_ffn2_w_down, l1_ffn1_norm, l1_ffn1_w_gate, l1_ffn1_w_up, l1_ffn1_w_down, l1_mix_norm, l1_w_in, l1_cmp_pe_k, l1_cmp_w1_k, l1_cmp_w2_k, l1_cmp_pe_v, l1_cmp_w1_v, l1_cmp_w2_v, l1_w_out, l1_ffn2_norm, l1_ffn2_w_gate, l1_ffn2_w_up, l1_ffn2_w_down, final_norm):
    b, s, d = x.shape
    m = b * s
    bf = lambda w: w.astype(BF16)
    x2 = x.reshape(m, d)

    x2 = _ffn(x2, l0_ffn1_norm, l0_ffn1_w_gate, l0_ffn1_w_up, l0_ffn1_w_down)
    z = _norm_matmul(x2, l0_mix_norm, bf(l0_w_in)).reshape(b, s, -1)
    cat = _l0_mid(z, l0_conv_w, l0_conv_b, l0_conv_ln_g, l0_conv_ln_b,
                  l0_gmlp_ln_g, l0_gmlp_ln_b, l0_gmlp_ws, l0_gmlp_bs)
    x2 = _matmul_res(cat.reshape(m, -1), bf(l0_w_out), x2)
    x2 = _ffn(x2, l0_ffn2_norm, l0_ffn2_w_gate, l0_ffn2_w_up, l0_ffn2_w_down)

    x2 = _ffn(x2, l1_ffn1_norm, l1_ffn1_w_gate, l1_ffn1_w_up, l1_ffn1_w_down)
    w_in = jnp.pad(bf(l1_w_in), ((0, 0), (0, NSA_PAD - NSA_MAIN - NSA_GATES)))
    z = _norm_matmul(x2, l1_mix_norm, w_in, n_col_tiles=11).reshape(b, s, -1)
    rope_c, rope_a, rope_b = _rope_tables(positions)
    ksr, vst, kwr, vwt, kcmp, vcmpt = _nsa_prep(
        z, rope_c, rope_a, rope_b, l1_cmp_pe_k, l1_cmp_w1_k, l1_cmp_w2_k,
        l1_cmp_pe_v, l1_cmp_w1_v, l1_cmp_w2_v)
    o = _nsa_attn(z, rope_c, rope_a, rope_b, ksr, vst, kwr, vwt, kcmp, vcmpt)
    x2 = _matmul_res(o.reshape(m, -1), bf(l1_w_out), x2)
    x2 = _ffn(x2, l1_ffn2_norm, l1_ffn2_w_gate, l1_ffn2_w_up, l1_ffn2_w_down,
              final_w=final_norm)
    return x2.reshape(b, s, d)
```

```python
import functools

import numpy as np
import jax
import jax.numpy as jnp
from jax import lax
from jax.experimental import pallas as pl
from jax.experimental.pallas import tpu as pltpu

F32 = jnp.float32
BF16 = jnp.bfloat16

D_MODEL = 2048
D_FF = 5632
EPS = 1e-6
CONV_CH = 1024
CONV_WIDTH = 31
GMLP_CH = 1024
GMLP_GROUPS = 8
GMLP_GDIM = 128
GMLP_CHUNK = 128
HEAD_DIM = 128
N_HEADS = 16
N_KV = 4
HPG = 4
KV_W = N_KV * HEAD_DIM
ROT_DIM = 32
ROPE_THETA = 500000.0
CMP_LEN = 32
CMP_STRIDE = 16
CMP_HIDDEN = 256
SEL_BLOCK = 64
SEL_TOPK = 16
WINDOW = 512
FORCE_BONUS = 1e3
NEG = -1e30
NSA_MAIN = D_MODEL + 6 * KV_W
NSA_GATES = 3 * N_HEADS
NSA_PAD = 5632
SCALE = HEAD_DIM ** -0.5

MIB = 1024 * 1024


def _cparams(semantics, vmem_mib):
    return pltpu.CompilerParams(dimension_semantics=semantics,
                                vmem_limit_bytes=vmem_mib * MIB)


def _rms_scale(x):
    return x * lax.rsqrt(jnp.mean(x * x, axis=-1, keepdims=True) + EPS)


DOWN_TN = 512


def _ffn_kernel(x_ref, nw_ref, wg_ref, wu_ref, wd_ref, *rest, final):
    if final:
        fn_ref, o_ref, h_scr = rest
    else:
        o_ref, h_scr = rest
    f = pl.program_id(1)

    @pl.when(f == 0)
    def _():
        h_scr[...] = (_rms_scale(x_ref[...]) * nw_ref[...]).astype(BF16)
        o_ref[...] = jnp.zeros(o_ref.shape, F32)

    h = h_scr[...]
    g = jnp.dot(h, wg_ref[...].astype(BF16), preferred_element_type=F32)
    u = jnp.dot(h, wu_ref[...].astype(BF16), preferred_element_type=F32)
    a = (g * jax.nn.sigmoid(g) * u).astype(BF16)
    for c in range(0, o_ref.shape[1], DOWN_TN):
        o_ref[:, c:c + DOWN_TN] += jnp.dot(a, wd_ref[:, c:c + DOWN_TN].astype(BF16),
                                           preferred_element_type=F32)

    @pl.when(f == pl.num_programs(1) - 1)
    def _():
        y = x_ref[...] + 0.5 * o_ref[...]
        if final:
            y = _rms_scale(y) * fn_ref[...]
        o_ref[...] = y


def _ffn(x2d, norm_w, wg, wu, wd, final_w=None, tm=1024, tf=256):
    m, d = x2d.shape
    dff = wg.shape[1]
    final = final_w is not None
    in_specs = [
        pl.BlockSpec((tm, d), lambda i, j: (i, 0)),
        pl.BlockSpec((1, d), lambda i, j: (0, 0)),
        pl.BlockSpec((d, tf), lambda i, j: (0, j)),
        pl.BlockSpec((d, tf), lambda i, j: (0, j)),
        pl.BlockSpec((tf, d), lambda i, j: (j, 0)),
    ]
    args = [x2d, norm_w.reshape(1, d), wg, wu, wd]
    if final:
        in_specs.append(pl.BlockSpec((1, d), lambda i, j: (0, 0)))
        args.append(final_w.reshape(1, d))
    return pl.pallas_call(
        functools.partial(_ffn_kernel, final=final),
        grid=(m // tm, dff // tf),
        in_specs=in_specs,
        out_specs=pl.BlockSpec((tm, d), lambda i, j: (i, 0)),
        out_shape=jax.ShapeDtypeStruct((m, d), F32),
        scratch_shapes=[pltpu.VMEM((tm, d), BF16)],
        compiler_params=_cparams(("parallel", "arbitrary"), 58),
        name="ffn",
    )(*args)


def _norm_matmul_kernel(x_ref, nw_ref, w_ref, o_ref, h_scr):
    @pl.when(pl.program_id(1) == 0)
    def _():
        h_scr[...] = (_rms_scale(x_ref[...]) * nw_ref[...]).astype(BF16)

    o_ref[...] = jnp.dot(h_scr[...], w_ref[...], preferred_element_type=F32)


def _norm_matmul(x2d, norm_w, w, n_col_tiles, vmem_mib, tm=1024):
    m, d = x2d.shape
    n = w.shape[1]
    tn = n // n_col_tiles
    assert tn * n_col_tiles == n and tn % HEAD_DIM == 0
    return pl.pallas_call(
        _norm_matmul_kernel,
        grid=(m // tm, n // tn),
        in_specs=[
            pl.BlockSpec((tm, d), lambda i, j: (i, 0)),
            pl.BlockSpec((1, d), lambda i, j: (0, 0)),
            pl.BlockSpec((d, tn), lambda i, j: (0, j)),
        ],
        out_specs=pl.BlockSpec((tm, tn), lambda i, j: (i, j)),
        out_shape=jax.ShapeDtypeStruct((m, n), F32),
        scratch_shapes=[pltpu.VMEM((tm, d), BF16)],
        compiler_params=_cparams(("parallel", "arbitrary"), vmem_mib),
        name="norm_matmul",
    )(x2d, norm_w.reshape(1, d), w)


def _matmul_res_kernel(a_ref, w_ref, r_ref, o_ref):
    o_ref[...] = r_ref[...] + jnp.dot(a_ref[...], w_ref[...], preferred_element_type=F32)


def _matmul_res(a, w, res, tm=512, tn=1024):
    m, k = a.shape
    n = w.shape[1]
    return pl.pallas_call(
        _matmul_res_kernel,
        grid=(m // tm, n // tn),
        in_specs=[
            pl.BlockSpec((tm, k), lambda i, j: (i, 0)),
            pl.BlockSpec((k, tn), lambda i, j: (0, j)),
            pl.BlockSpec((tm, tn), lambda i, j: (i, j)),
        ],
        out_specs=pl.BlockSpec((tm, tn), lambda i, j: (i, j)),
        out_shape=jax.ShapeDtypeStruct((m, n), F32),
        compiler_params=_cparams(("parallel", "arbitrary"), 24),
        name="matmul_res",
    )(a, w, res)


CONV_HALO = 32
SUBLANES = 8


def _layernorm(x, g, b):
    mu = jnp.mean(x, axis=-1, keepdims=True)
    xc = x - mu
    var = jnp.mean(xc * xc, axis=-1, keepdims=True)
    return xc * lax.rsqrt(var + EPS) * g + b


def _l0_mid_kernel(av_ref, ag_ref, bu_ref, bv_ref, cw_ref, cb_ref, clg_ref, clb_ref,
                   glg_ref, glb_ref, ws_ref, bst_ref, o_ref, abuf, shifted, *, ts):
    s = pl.program_id(1)

    @pl.when(s == 0)
    def _():
        abuf[0:CONV_HALO, :] = jnp.zeros((CONV_HALO, CONV_CH), F32)

    abuf[CONV_HALO:CONV_HALO + ts, :] = av_ref[0] * jax.nn.sigmoid(ag_ref[0])
    span = CONV_HALO + ts - SUBLANES
    for r in range(1, SUBLANES):
        shifted[r - 1] = abuf[r:r + span, :]
    base = CONV_HALO - (CONV_WIDTH - 1)
    acc = jnp.zeros((ts, CONV_CH), F32)
    for j in range(CONV_WIDTH):
        q, r = divmod(base + j, SUBLANES)
        src = abuf if r == 0 else shifted.at[r - 1]
        rows = src[SUBLANES * q:SUBLANES * q + ts, :]
        acc = acc + rows * cw_ref[j:j + 1, :]
    acc = acc + cb_ref[...]
    a = _layernorm(acc, clg_ref[...], clb_ref[...])
    o_ref[0, :, 0:CONV_CH] = (a * jax.nn.sigmoid(a)).astype(o_ref.dtype)
    abuf[0:CONV_HALO, :] = abuf[ts:ts + CONV_HALO, :]

    row = lax.broadcasted_iota(jnp.int32, (GMLP_CHUNK, GMLP_CHUNK), 0)
    col = lax.broadcasted_iota(jnp.int32, (GMLP_CHUNK, GMLP_CHUNK), 1)
    tri = col <= row
    for g in range(GMLP_GROUPS):
        lo = g * GMLP_GDIM
        w_g = jnp.where(tri, ws_ref[g], 0.0).astype(BF16)
        bias_g = bst_ref[:, g:g + 1]
        ln_g = glg_ref[:, lo:lo + GMLP_GDIM]
        ln_b = glb_ref[:, lo:lo + GMLP_GDIM]
        for c in range(ts // GMLP_CHUNK):
            r0 = c * GMLP_CHUNK
            u = jax.nn.gelu(bu_ref[0, r0:r0 + GMLP_CHUNK, lo:lo + GMLP_GDIM])
            v = jax.nn.gelu(bv_ref[0, r0:r0 + GMLP_CHUNK, lo:lo + GMLP_GDIM])
            v = _layernorm(v, ln_g, ln_b)
            sv = jnp.dot(w_g, v.astype(BF16), preferred_element_type=F32) + bias_g
            o_ref[0, r0:r0 + GMLP_CHUNK, CONV_CH + lo:CONV_CH + lo + GMLP_GDIM] = (
                (u * sv).astype(o_ref.dtype))


def _l0_mid(z, conv_w, conv_b, conv_ln_g, conv_ln_b, gmlp_ln_g, gmlp_ln_b, gmlp_ws, gmlp_bs, ts=256):
    b, s, _ = z.shape
    vec = lambda a: a.reshape(1, -1)
    const2 = lambda shape: pl.BlockSpec(shape, lambda i, j: (0, 0))
    zspec = lambda k: pl.BlockSpec((1, ts, CONV_CH), lambda i, j, k=k: (i, j, k))
    return pl.pallas_call(
        functools.partial(_l0_mid_kernel, ts=ts),
        grid=(b, s // ts),
        in_specs=[
            zspec(0), zspec(1), zspec(2), zspec(3),
            const2((CONV_WIDTH, CONV_CH)),
            const2((1, CONV_CH)), const2((1, CONV_CH)), const2((1, CONV_CH)),
            const2((1, GMLP_CH)), const2((1, GMLP_CH)),
            pl.BlockSpec((GMLP_GROUPS, GMLP_CHUNK, GMLP_CHUNK), lambda i, j: (0, 0, 0)),
            const2((GMLP_CHUNK, GMLP_GROUPS)),
        ],
        out_specs=pl.BlockSpec((1, ts, CONV_CH + GMLP_CH), lambda i, j: (i, j, 0)),
        out_shape=jax.ShapeDtypeStruct((b, s, CONV_CH + GMLP_CH), BF16),
        scratch_shapes=[pltpu.VMEM((CONV_HALO + ts, CONV_CH), F32),
                        pltpu.VMEM((SUBLANES - 1, CONV_HALO + ts - SUBLANES, CONV_CH), F32)],
        compiler_params=_cparams(("parallel", "arbitrary"), 16),
        name="l0_mid",
    )(z, z, z, z, conv_w.reshape(CONV_WIDTH, CONV_CH), vec(conv_b), vec(conv_ln_g), vec(conv_ln_b),
      vec(gmlp_ln_g), vec(gmlp_ln_b), gmlp_ws, gmlp_bs.T)


SEL_TK = 512
WIN_TK = 128
N_HALF = 128


def _rope(x, c, a, b):
    return x * c + pltpu.roll(x, HEAD_DIM - ROT_DIM // 2, 1) * a + pltpu.roll(x, ROT_DIM // 2, 1) * b


def _compress(k_ref, pe_ref, w1_ref, w2_ref):
    half = CMP_LEN // 2
    p = jnp.zeros((N_HALF, CMP_HIDDEN), F32)
    q = jnp.zeros((N_HALF, CMP_HIDDEN), F32)
    for l in range(half):
        rows = k_ref[0, pl.ds(l, N_HALF, stride=CMP_STRIDE), :]
        p = p + jnp.dot((rows + pe_ref[l:l + 1, :]).astype(BF16), w1_ref[l],
                        preferred_element_type=F32)
        q = q + jnp.dot((rows + pe_ref[half + l:half + l + 1, :]).astype(BF16), w1_ref[half + l],
                        preferred_element_type=F32)
    hid = p + pltpu.roll(q, N_HALF - 1, 0)
    return jnp.dot(jax.nn.gelu(hid).astype(BF16), w2_ref[...], preferred_element_type=F32)


def _nsa_prep_kernel(kc_ref, vc_ref, ks_ref, vs_ref, kw_ref, vw_ref, rc_ref, ra_ref, rb_ref,
                     pek_ref, w1k_ref, w2k_ref, pev_ref, w1v_ref, w2v_ref,
                     ksr_ref, vst_ref, kwr_ref, vwt_ref, kcmp_ref, vcmpt_ref):
    c, a, b = rc_ref[0], ra_ref[0], rb_ref[0]
    ksr_ref[0, 0] = _rope(ks_ref[0], c, a, b).astype(BF16)
    kwr_ref[0, 0] = _rope(kw_ref[0], c, a, b).astype(BF16)
    s = vs_ref.shape[1]
    for t in range(s // SEL_TK):
        vst_ref[0, 0, t] = vs_ref[0, t * SEL_TK:(t + 1) * SEL_TK, :].T.astype(BF16)
    for t in range(s // WIN_TK):
        vwt_ref[0, 0, t] = vw_ref[0, t * WIN_TK:(t + 1) * WIN_TK, :].T.astype(BF16)
    kcmp_ref[0, 0] = _compress(kc_ref, pek_ref, w1k_ref, w2k_ref).astype(BF16)
    vcmpt_ref[0, 0] = _compress(vc_ref, pev_ref, w1v_ref, w2v_ref).T.astype(BF16)


def _nsa_prep(z, rope_c, rope_a, rope_b, pe_k, w1_k, w2_k, pe_v, w1_v, w2_v):
    b, s, _ = z.shape
    first = D_MODEL // HEAD_DIM
    zspec = lambda k: pl.BlockSpec((1, s, HEAD_DIM), lambda i, g, k=k: (i, 0, first + k * N_KV + g))
    tab = pl.BlockSpec((1, s, HEAD_DIM), lambda i, g: (i, 0, 0))
    c2 = lambda shape: pl.BlockSpec(shape, lambda i, g: (0, 0))
    c3 = lambda shape: pl.BlockSpec(shape, lambda i, g: (0, 0, 0))
    out4 = lambda shape: pl.BlockSpec((1, 1) + shape, lambda i, g: (i, g, 0, 0))
    out5 = lambda shape: pl.BlockSpec((1, 1) + shape, lambda i, g: (i, g, 0, 0, 0))
    w1 = lambda w: w.reshape(CMP_LEN, HEAD_DIM, CMP_HIDDEN)
    return pl.pallas_call(
        _nsa_prep_kernel,
        grid=(b, N_KV),
        in_specs=[zspec(0), zspec(1), zspec(2), zspec(3), zspec(4), zspec(5), tab, tab, tab,
                  c2((CMP_LEN, HEAD_DIM)), c3((CMP_LEN, HEAD_DIM, CMP_HIDDEN)), c2((CMP_HIDDEN, HEAD_DIM)),
                  c2((CMP_LEN, HEAD_DIM)), c3((CMP_LEN, HEAD_DIM, CMP_HIDDEN)), c2((CMP_HIDDEN, HEAD_DIM))],
        out_specs=[out4((s, HEAD_DIM)), out5((s // SEL_TK, HEAD_DIM, SEL_TK)),
                   out4((s, HEAD_DIM)), out5((s // WIN_TK, HEAD_DIM, WIN_TK)),
                   out4((N_HALF, HEAD_DIM)), out4((HEAD_DIM, N_HALF))],
        out_shape=[jax.ShapeDtypeStruct((b, N_KV, s, HEAD_DIM), BF16),
                   jax.ShapeDtypeStruct((b, N_KV, s // SEL_TK, HEAD_DIM, SEL_TK), BF16),
                   jax.ShapeDtypeStruct((b, N_KV, s, HEAD_DIM), BF16),
                   jax.ShapeDtypeStruct((b, N_KV, s // WIN_TK, HEAD_DIM, WIN_TK), BF16),
                   jax.ShapeDtypeStruct((b, N_KV, N_HALF, HEAD_DIM), BF16),
                   jax.ShapeDtypeStruct((b, N_KV, HEAD_DIM, N_HALF), BF16)],
        compiler_params=_cparams(("parallel", "arbitrary"), 22),
        name="nsa_prep",
    )(z, z, z, z, z, z, rope_c, rope_a, rope_b,
      pe_k, w1(w1_k).astype(BF16), w2_k.astype(BF16), pe_v, w1(w1_v).astype(BF16), w2_v.astype(BF16))


Q_SCALE = SCALE * 1.4426950408889634


def _scores_t(k, q):
    return lax.dot_general(k, q, (((1,), (1,)), ((), ())), preferred_element_type=F32)


def _online_attend(q, k_ref, vt_ref, sel, tpos, tile, n_tiles):
    rows = q.shape[0]
    m = jnp.full((1, rows), NEG, F32)
    l = jnp.zeros((1, rows), F32)
    acc = jnp.zeros((HEAD_DIM, rows), F32)
    for kt in range(n_tiles):
        blocks = tile // SEL_BLOCK
        sel_rows = sel[kt * blocks:(kt + 1) * blocks, :]
        visible = jnp.broadcast_to(sel_rows[:, None, :], (blocks, SEL_BLOCK, sel.shape[1])
                                   ).reshape(tile, sel.shape[1]) > 0.5
        if kt == n_tiles - 1:
            kpos = kt * tile + lax.broadcasted_iota(jnp.int32, visible.shape, 0)
            visible = visible & (kpos <= tpos)
        bias = jnp.where(visible, 0.0, NEG)
        s = (_scores_t(k_ref[0, 0, kt * tile:(kt + 1) * tile, :], q)
             + jnp.concatenate([bias] * (rows // bias.shape[1]), axis=1))
        m_new = jnp.maximum(m, jnp.max(s, axis=0, keepdims=True))
        alpha = jnp.exp2(m - m_new)
        p = jnp.exp2(s - m_new)
        l = alpha * l + jnp.sum(p, axis=0, keepdims=True)
        acc = alpha * acc + jnp.dot(vt_ref[0, 0, kt], p.astype(BF16), preferred_element_type=F32)
        m = m_new
    return acc / l


def _nsa_attn_kernel(q_ref, zg_ref, rc_ref, ra_ref, rb_ref, ksr_ref, vst_ref, kwr_ref, vwt_ref,
                     kcmp_ref, vcmpt_ref, ovt_ref, o_ref, gate_scr, osel_scr, *, tq, seq):
    grp = pl.program_id(1)
    t0 = pl.program_id(2) * tq
    rows = HPG * tq
    heads = lambda x: jnp.concatenate([x] * HPG, axis=1)

    c, a, b = rc_ref[0], ra_ref[0], rb_ref[0]
    q_plain, q_rot = [], []
    for h in range(HPG):
        qh = q_ref[0, :, h * HEAD_DIM:(h + 1) * HEAD_DIM] * Q_SCALE
        q_plain.append(qh.astype(BF16))
        q_rot.append(_rope(qh, c, a, b).astype(BF16))
    q_plain = jnp.concatenate(q_plain, axis=0)
    q_rot = jnp.concatenate(q_rot, axis=0)

    tpos = t0 + lax.broadcasted_iota(jnp.int32, (1, tq), 1)

    s_c = _scores_t(kcmp_ref[0, 0], q_plain)
    n_idx = lax.broadcasted_iota(jnp.int32, (N_HALF, rows), 0)
    cmask = n_idx * CMP_STRIDE + (CMP_LEN - 1) <= heads(tpos)
    s_c = jnp.where(cmask, s_c, NEG)
    e = jnp.exp2(s_c - jnp.max(s_c, axis=0, keepdims=True))
    p_c = jnp.where(cmask, e / jnp.sum(e, axis=0, keepdims=True), 0.0)
    o_c = jnp.dot(vcmpt_ref[0, 0], p_c.astype(BF16), preferred_element_type=F32)

    span = WINDOW + tq
    k0 = pl.multiple_of(jnp.maximum(t0 - WINDOW, 0), WIN_TK)
    kp = k0 + lax.broadcasted_iota(jnp.int32, (span, tq), 0)
    w_bias = jnp.where((kp <= tpos) & (kp > tpos - WINDOW), 0.0, NEG)
    s_w = _scores_t(kwr_ref[0, 0, pl.ds(k0, span), :], q_rot) + heads(w_bias)
    p_w = jnp.exp2(s_w - jnp.max(s_w, axis=0, keepdims=True))
    l_w = jnp.sum(p_w, axis=0, keepdims=True)
    p_w = p_w.astype(BF16)
    o_w = jnp.zeros((HEAD_DIM, rows), F32)
    for i in range(span // WIN_TK):
        o_w = o_w + jnp.dot(vwt_ref[0, 0, k0 // WIN_TK + i], p_w[i * WIN_TK:(i + 1) * WIN_TK, :],
                            preferred_element_type=F32)
    o_w = o_w / l_w

    p_sum = p_c[:, 0:tq]
    for h in range(1, HPG):
        p_sum = p_sum + p_c[:, h * tq:(h + 1) * tq]
    p_hi = p_sum.astype(BF16)
    p_lo = (p_sum - p_hi.astype(F32)).astype(BF16)
    ovt = ovt_ref[...]
    imp = (jnp.dot(ovt, p_hi, preferred_element_type=F32)
           + jnp.dot(ovt, p_lo, preferred_element_type=F32))
    n_blk = seq // SEL_BLOCK
    j_idx = lax.broadcasted_iota(jnp.int32, (n_blk, tq), 0)
    cur = tpos // SEL_BLOCK
    forced = (j_idx == 0) | (j_idx == cur) | (j_idx == cur - 1)
    imp = imp + jnp.where(forced, FORCE_BONUS, 0.0)
    imp = jnp.where(j_idx * SEL_BLOCK <= tpos, imp, NEG)
    rank = jnp.zeros((n_blk, tq), jnp.int32)
    for i in range(n_blk):
        vi = imp[i:i + 1, :]
        ahead = (vi > imp) | ((vi == imp) & (j_idx > i))
        rank = rank + ahead.astype(jnp.int32)
    sel = (rank < min(SEL_TOPK, n_blk)).astype(F32)

    n_tiles =(t0 + tq + SEL_TK - 1) // SEL_TK
    for n in range(1, seq // SEL_TK + 1):
        @pl.when(n_tiles == n)
        def _(n=n):
            osel_scr[...] = _online_attend(q_rot, ksr_ref, vst_ref, sel, tpos, SEL_TK, n)
    o_s = osel_scr[...]

    gate_scr[...] = jax.nn.sigmoid(zg_ref[0]).T

    def gate(branch):
        return jnp.concatenate(
            [gate_scr[pl.ds(3 * (HPG * grp + h) + branch, 1), :] for h in range(HPG)], axis=1)

    o = gate(0) * o_c + gate(1) * o_s + gate(2) * o_w
    for h in range(HPG):
        o_ref[0, :, h * HEAD_DIM:(h + 1) * HEAD_DIM] = o[:, h * tq:(h + 1) * tq].T.astype(o_ref.dtype)


def _nsa_attn(z, rope_c, rope_a, rope_b, ksr, vst, kwr, vwt, kcmp, vcmpt, tq=256):
    b, s, _ = z.shape
    assert SEL_TK % tq == 0 and s % SEL_TK == 0
    n_blk = s // SEL_BLOCK
    n = np.arange(N_HALF)[None, :]
    j = np.arange(n_blk)[:, None]
    ovt = ((n < N_HALF - 1) & (n * CMP_STRIDE < (j + 1) * SEL_BLOCK)
           & (n * CMP_STRIDE + CMP_LEN - 1 >= j * SEL_BLOCK))
    gate_blk = NSA_MAIN // HEAD_DIM
    tab = pl.BlockSpec((1, tq, HEAD_DIM), lambda i, g, t: (i, t, 0))
    kv4 = lambda shape: pl.BlockSpec((1, 1) + shape, lambda i, g, t: (i, g, 0, 0))
    kv5 = lambda shape: pl.BlockSpec((1, 1) + shape, lambda i, g, t: (i, g, 0, 0, 0))
    return pl.pallas_call(
        functools.partial(_nsa_attn_kernel, tq=tq, seq=s),
        grid=(b, N_KV, s // tq),
        in_specs=[
            pl.BlockSpec((1, tq, HPG * HEAD_DIM), lambda i, g, t: (i, t, g)),
            pl.BlockSpec((1, tq, HEAD_DIM), lambda i, g, t: (i, t, gate_blk)),
            tab, tab, tab,
            kv4((s, HEAD_DIM)), kv5((s // SEL_TK, HEAD_DIM, SEL_TK)),
            kv4((s, HEAD_DIM)), kv5((s // WIN_TK, HEAD_DIM, WIN_TK)),
            kv4((N_HALF, HEAD_DIM)), kv4((HEAD_DIM, N_HALF)),
            pl.BlockSpec((n_blk, N_HALF), lambda i, g, t: (0, 0)),
        ],
        out_specs=pl.BlockSpec((1, tq, HPG * HEAD_DIM), lambda i, g, t: (i, t, g)),
        out_shape=jax.ShapeDtypeStruct((b, s, D_MODEL), BF16),
        scratch_shapes=[pltpu.VMEM((HEAD_DIM, tq), F32), pltpu.VMEM((HEAD_DIM, HPG * tq), F32)],
        compiler_params=_cparams(("parallel", "parallel", "arbitrary"), 24),
        name="nsa_attn",
    )(z, z, rope_c, rope_a, rope_b, ksr, vst, kwr, vwt, kcmp, vcmpt,
      jnp.asarray(ovt, BF16))


def _rope_tables(positions):
    half = ROT_DIM // 2
    inv = ROPE_THETA ** (-jnp.arange(0, ROT_DIM, 2, dtype=F32) / ROT_DIM)
    ang = positions.astype(F32)[..., None] * inv
    cos, sin = jnp.cos(ang), jnp.sin(ang)
    zeros = lambda w: jnp.zeros(cos.shape[:-1] + (w,), F32)
    c = jnp.concatenate([cos, cos, jnp.ones(cos.shape[:-1] + (HEAD_DIM - ROT_DIM,), F32)], axis=-1)
    a = jnp.concatenate([-sin, zeros(HEAD_DIM - half)], axis=-1)
    b = jnp.concatenate([zeros(half), sin, zeros(HEAD_DIM - ROT_DIM)], axis=-1)
    return c, a, b


def kernel(x, positions, l0_ffn1_norm, l0_ffn1_w_gate, l0_ffn1_w_up, l0_ffn1_w_down, l0_mix_norm, l0_w_in, l0_conv_w, l0_conv_b, l0_conv_ln_g, l0_conv_ln_b, l0_gmlp_ln_g, l0_gmlp_ln_b, l0_gmlp_ws, l0_gmlp_bs, l0_w_out, l0_ffn2_norm, l0_ffn2_w_gate, l0_ffn2_w_up, l0_ffn2_w_down, l1_ffn1_norm, l1_ffn1_w_gate, l1_ffn1_w_up, l1_ffn1_w_down, l1_mix_norm, l1_w_in, l1_cmp_pe_k, l1_cmp_w1_k, l1_cmp_w2_k, l1_cmp_pe_v, l1_cmp_w1_v, l1_cmp_w2_v, l1_w_out, l1_ffn2_norm, l1_ffn2_w_gate, l1_ffn2_w_up, l1_ffn2_w_down, final_norm):
    b, s, d = x.shape
    m = b * s
    bf = lambda w: w.astype(BF16)
    x2 = x.reshape(m, d)

    x2 = _ffn(x2, l0_ffn1_norm, l0_ffn1_w_gate, l0_ffn1_w_up, l0_ffn1_w_down)
    z = _norm_matmul(x2, l0_mix_norm, bf(l0_w_in), n_col_tiles=8, vmem_mib=30).reshape(b, s, -1)
    cat = _l0_mid(z, l0_conv_w, l0_conv_b, l0_conv_ln_g, l0_conv_ln_b,
                  l0_gmlp_ln_g, l0_gmlp_ln_b, l0_gmlp_ws, l0_gmlp_bs)
    x2 = _matmul_res(cat.reshape(m, -1), bf(l0_w_out), x2)
    x2 = _ffn(x2, l0_ffn2_norm, l0_ffn2_w_gate, l0_ffn2_w_up, l0_ffn2_w_down)

    x2 = _ffn(x2, l1_ffn1_norm, l1_ffn1_w_gate, l1_ffn1_w_up, l1_ffn1_w_down)
    w_in = jnp.pad(bf(l1_w_in), ((0, 0), (0, NSA_PAD - NSA_MAIN - NSA_GATES)))
    z = _norm_matmul(x2, l1_mix_norm, w_in, n_col_tiles=11, vmem_mib=30).reshape(b, s, -1)
    rope_c, rope_a, rope_b = _rope_tables(positions)
    ksr, vst, kwr, vwt, kcmp, vcmpt = _nsa_prep(
        z, rope_c, rope_a, rope_b, l1_cmp_pe_k, l1_cmp_w1_k, l1_cmp_w2_k,
        l1_cmp_pe_v, l1_cmp_w1_v, l1_cmp_w2_v)
    o = _nsa_attn(z, rope_c, rope_a, rope_b, ksr, vst, kwr, vwt, kcmp, vcmpt)
    x2 = _matmul_res(o.reshape(m, -1), bf(l1_w_out), x2)
    x2 = _ffn(x2, l1_ffn2_norm, l1_ffn2_w_gate, l1_ffn2_w_up, l1_ffn2_w_down,
              final_w=final_norm)
    return x2.reshape(b, s, d)
```

```python
import functools

import numpy as np
import jax
import jax.numpy as jnp
from jax import lax
from jax.experimental import pallas as pl
from jax.experimental.pallas import tpu as pltpu

F32 = jnp.float32
BF16 = jnp.bfloat16

D_MODEL = 2048
D_FF = 5632
EPS = 1e-6
CONV_CH = 1024
CONV_WIDTH = 31
GMLP_CH = 1024
GMLP_GROUPS = 8
GMLP_GDIM = 128
GMLP_CHUNK = 128
HEAD_DIM = 128
N_HEADS = 16
N_KV = 4
HPG = 4
KV_W = N_KV * HEAD_DIM
ROT_DIM = 32
ROPE_THETA = 500000.0
CMP_LEN = 32
CMP_STRIDE = 16
CMP_HIDDEN = 256
SEL_BLOCK = 64
SEL_TOPK = 16
WINDOW = 512
FORCE_BONUS = 1e3
NEG = -1e30
NSA_MAIN = D_MODEL + 6 * KV_W
NSA_GATES = 3 * N_HEADS
NSA_PAD = 5632
SCALE = HEAD_DIM ** -0.5

MIB = 1024 * 1024


def _cparams(semantics, vmem_mib):
    return pltpu.CompilerParams(dimension_semantics=semantics,
                                vmem_limit_bytes=vmem_mib * MIB)


def _rms_scale(x):
    return x * lax.rsqrt(jnp.mean(x * x, axis=-1, keepdims=True) + EPS)


DOWN_TN = 512


def _ffn_kernel(x_ref, nw_ref, wg_ref, wu_ref, wd_ref, *rest, final):
    if final:
        fn_ref, o_ref, h_scr = rest
    else:
        o_ref, h_scr = rest
    f = pl.program_id(1)

    @pl.when(f == 0)
    def _():
        h_scr[...] = (_rms_scale(x_ref[...]) * nw_ref[...]).astype(BF16)
        o_ref[...] = jnp.zeros(o_ref.shape, F32)

    h = h_scr[...]
    g = jnp.dot(h, wg_ref[...].astype(BF16), preferred_element_type=F32)
    u = jnp.dot(h, wu_ref[...].astype(BF16), preferred_element_type=F32)
    a = (g * jax.nn.sigmoid(g) * u).astype(BF16)
    for c in range(0, o_ref.shape[1], DOWN_TN):
        o_ref[:, c:c + DOWN_TN] += jnp.dot(a, wd_ref[:, c:c + DOWN_TN].astype(BF16),
                                           preferred_element_type=F32)

    @pl.when(f == pl.num_programs(1) - 1)
    def _():
        y = x_ref[...] + 0.5 * o_ref[...]
        if final:
            y = _rms_scale(y) * fn_ref[...]
        o_ref[...] = y


def _ffn(x2d, norm_w, wg, wu, wd, final_w=None, tm=1024, tf=256):
    m, d = x2d.shape
    dff = wg.shape[1]
    final = final_w is not None
    in_specs = [
        pl.BlockSpec((tm, d), lambda i, j: (i, 0)),
        pl.BlockSpec((1, d), lambda i, j: (0, 0)),
        pl.BlockSpec((d, tf), lambda i, j: (0, j)),
        pl.BlockSpec((d, tf), lambda i, j: (0, j)),
        pl.BlockSpec((tf, d), lambda i, j: (j, 0)),
    ]
    args = [x2d, norm_w.reshape(1, d), wg, wu, wd]
    if final:
        in_specs.append(pl.BlockSpec((1, d), lambda i, j: (0, 0)))
        args.append(final_w.reshape(1, d))
    return pl.pallas_call(
        functools.partial(_ffn_kernel, final=final),
        grid=(m // tm, dff // tf),
        in_specs=in_specs,
        out_specs=pl.BlockSpec((tm, d), lambda i, j: (i, 0)),
        out_shape=jax.ShapeDtypeStruct((m, d), F32),
        scratch_shapes=[pltpu.VMEM((tm, d), BF16)],
        compiler_params=_cparams(("parallel", "arbitrary"), 58),
        name="ffn",
    )(*args)


def _norm_matmul_kernel(x_ref, nw_ref, w_ref, o_ref, h_scr):
    @pl.when(pl.program_id(1) == 0)
    def _():
        h_scr[...] = (_rms_scale(x_ref[...]) * nw_ref[...]).astype(BF16)

    o_ref[...] = jnp.dot(h_scr[...], w_ref[...], preferred_element_type=F32)


def _norm_matmul(x2d, norm_w, w, n_col_tiles, vmem_mib, tm=1024):
    m, d = x2d.shape
    n = w.shape[1]
    tn = n // n_col_tiles
    assert tn * n_col_tiles == n and tn % HEAD_DIM == 0
    return pl.pallas_call(
        _norm_matmul_kernel,
        grid=(m // tm, n // tn),
        in_specs=[
            pl.BlockSpec((tm, d), lambda i, j: (i, 0)),
            pl.BlockSpec((1, d), lambda i, j: (0, 0)),
            pl.BlockSpec((d, tn), lambda i, j: (0, j)),
        ],
        out_specs=pl.BlockSpec((tm, tn), lambda i, j: (i, j)),
        out_shape=jax.ShapeDtypeStruct((m, n), F32),
        scratch_shapes=[pltpu.VMEM((tm, d), BF16)],
        compiler_params=_cparams(("parallel", "arbitrary"), vmem_mib),
        name="norm_matmul",
    )(x2d, norm_w.reshape(1, d), w)


def _matmul_res_kernel(a_ref, w_ref, r_ref, o_ref):
    o_ref[...] = r_ref[...] + jnp.dot(a_ref[...], w_ref[...], preferred_element_type=F32)


def _matmul_res(a, w, res, tm=512, tn=1024):
    m, k = a.shape
    n = w.shape[1]
    return pl.pallas_call(
        _matmul_res_kernel,
        grid=(m // tm, n // tn),
        in_specs=[
            pl.BlockSpec((tm, k), lambda i, j: (i, 0)),
            pl.BlockSpec((k, tn), lambda i, j: (0, j)),
            pl.BlockSpec((tm, tn), lambda i, j: (i, j)),
        ],
        out_specs=pl.BlockSpec((tm, tn), lambda i, j: (i, j)),
        out_shape=jax.ShapeDtypeStruct((m, n), F32),
        compiler_params=_cparams(("parallel", "arbitrary"), 24),
        name="matmul_res",
    )(a, w, res)


CONV_HALO = 32
SUBLANES = 8


def _layernorm(x, g, b):
    mu = jnp.mean(x, axis=-1, keepdims=True)
    xc = x - mu
    var = jnp.mean(xc * xc, axis=-1, keepdims=True)
    return xc * lax.rsqrt(var + EPS) * g + b


def _l0_mid_kernel(av_ref, ag_ref, bu_ref, bv_ref, cw_ref, cb_ref, clg_ref, clb_ref,
                   glg_ref, glb_ref, ws_ref, bst_ref, o_ref, abuf, shifted, *, ts):
    s = pl.program_id(1)

    @pl.when(s == 0)
    def _():
        abuf[0:CONV_HALO, :] = jnp.zeros((CONV_HALO, CONV_CH), F32)

    abuf[CONV_HALO:CONV_HALO + ts, :] = av_ref[0] * jax.nn.sigmoid(ag_ref[0])
    span = CONV_HALO + ts - SUBLANES
    for r in range(1, SUBLANES):
        shifted[r - 1] = abuf[r:r + span, :]
    base = CONV_HALO - (CONV_WIDTH - 1)
    acc = jnp.zeros((ts, CONV_CH), F32)
    for j in range(CONV_WIDTH):
        q, r = divmod(base + j, SUBLANES)
        src = abuf if r == 0 else shifted.at[r - 1]
        rows = src[SUBLANES * q:SUBLANES * q + ts, :]
        acc = acc + rows * cw_ref[j:j + 1, :]
    acc = acc + cb_ref[...]
    a = _layernorm(acc, clg_ref[...], clb_ref[...])
    o_ref[0, :, 0:CONV_CH] = (a * jax.nn.sigmoid(a)).astype(o_ref.dtype)
    abuf[0:CONV_HALO, :] = abuf[ts:ts + CONV_HALO, :]

    row = lax.broadcasted_iota(jnp.int32, (GMLP_CHUNK, GMLP_CHUNK), 0)
    col = lax.broadcasted_iota(jnp.int32, (GMLP_CHUNK, GMLP_CHUNK), 1)
    tri = col <= row
    for g in range(GMLP_GROUPS):
        lo = g * GMLP_GDIM
        w_g = jnp.where(tri, ws_ref[g], 0.0).astype(BF16)
        bias_g = bst_ref[:, g:g + 1]
        ln_g = glg_ref[:, lo:lo + GMLP_GDIM]
        ln_b = glb_ref[:, lo:lo + GMLP_GDIM]
        for c in range(ts // GMLP_CHUNK):
            r0 = c * GMLP_CHUNK
            u = jax.nn.gelu(bu_ref[0, r0:r0 + GMLP_CHUNK, lo:lo + GMLP_GDIM])
            v = jax.nn.gelu(bv_ref[0, r0:r0 + GMLP_CHUNK, lo:lo + GMLP_GDIM])
            v = _layernorm(v, ln_g, ln_b)
            sv = jnp.dot(w_g, v.astype(BF16), preferred_element_type=F32) + bias_g
            o_ref[0, r0:r0 + GMLP_CHUNK, CONV_CH + lo:CONV_CH + lo + GMLP_GDIM] = (
                (u * sv).astype(o_ref.dtype))


def _l0_mid(z, conv_w, conv_b, conv_ln_g, conv_ln_b, gmlp_ln_g, gmlp_ln_b, gmlp_ws, gmlp_bs, ts=256):
    b, s, _ = z.shape
    vec = lambda a: a.reshape(1, -1)
    const2 = lambda shape: pl.BlockSpec(shape, lambda i, j: (0, 0))
    zspec = lambda k: pl.BlockSpec((1, ts, CONV_CH), lambda i, j, k=k: (i, j, k))
    return pl.pallas_call(
        functools.partial(_l0_mid_kernel, ts=ts),
        grid=(b, s // ts),
        in_specs=[
            zspec(0), zspec(1), zspec(2), zspec(3),
            const2((CONV_WIDTH, CONV_CH)),
            const2((1, CONV_CH)), const2((1, CONV_CH)), const2((1, CONV_CH)),
            const2((1, GMLP_CH)), const2((1, GMLP_CH)),
            pl.BlockSpec((GMLP_GROUPS, GMLP_CHUNK, GMLP_CHUNK), lambda i, j: (0, 0, 0)),
            const2((GMLP_CHUNK, GMLP_GROUPS)),
        ],
        out_specs=pl.BlockSpec((1, ts, CONV_CH + GMLP_CH), lambda i, j: (i, j, 0)),
        out_shape=jax.ShapeDtypeStruct((b, s, CONV_CH + GMLP_CH), BF16),
        scratch_shapes=[pltpu.VMEM((CONV_HALO + ts, CONV_CH), F32),
                        pltpu.VMEM((SUBLANES - 1, CONV_HALO + ts - SUBLANES, CONV_CH), F32)],
        compiler_params=_cparams(("parallel", "arbitrary"), 16),
        name="l0_mid",
    )(z, z, z, z, conv_w.reshape(CONV_WIDTH, CONV_CH), vec(conv_b), vec(conv_ln_g), vec(conv_ln_b),
      vec(gmlp_ln_g), vec(gmlp_ln_b), gmlp_ws, gmlp_bs.T)


SEL_TK = 512
WIN_TK = 128
N_HALF = 128


def _rope(x, c, a, b):
    return x * c + pltpu.roll(x, HEAD_DIM - ROT_DIM // 2, 1) * a + pltpu.roll(x, ROT_DIM // 2, 1) * b


def _compress(k_ref, pe_ref, w1_ref, w2_ref):
    half = CMP_LEN // 2
    p = jnp.zeros((N_HALF, CMP_HIDDEN), F32)
    q = jnp.zeros((N_HALF, CMP_HIDDEN), F32)
    for l in range(half):
        rows = k_ref[0, pl.ds(l, N_HALF, stride=CMP_STRIDE), :]
        p = p + jnp.dot((rows + pe_ref[l:l + 1, :]).astype(BF16), w1_ref[l],
                        preferred_element_type=F32)
        q = q + jnp.dot((rows + pe_ref[half + l:half + l + 1, :]).astype(BF16), w1_ref[half + l],
                        preferred_element_type=F32)
    hid = p + pltpu.roll(q, N_HALF - 1, 0)
    return jnp.dot(jax.nn.gelu(hid).astype(BF16), w2_ref[...], preferred_element_type=F32)


def _nsa_prep_kernel(kc_ref, vc_ref, ks_ref, vs_ref, kw_ref, vw_ref, rc_ref, ra_ref, rb_ref,
                     pek_ref, w1k_ref, w2k_ref, pev_ref, w1v_ref, w2v_ref,
                     ksr_ref, vst_ref, kwr_ref, vwt_ref, kcmp_ref, vcmpt_ref):
    c, a, b = rc_ref[0], ra_ref[0], rb_ref[0]
    ksr_ref[0, 0] = _rope(ks_ref[0], c, a, b).astype(BF16)
    kwr_ref[0, 0] = _rope(kw_ref[0], c, a, b).astype(BF16)
    s = vs_ref.shape[1]
    for t in range(s // SEL_TK):
        vst_ref[0, 0, t] = vs_ref[0, t * SEL_TK:(t + 1) * SEL_TK, :].T.astype(BF16)
    for t in range(s // WIN_TK):
        vwt_ref[0, 0, t] = vw_ref[0, t * WIN_TK:(t + 1) * WIN_TK, :].T.astype(BF16)
    kcmp_ref[0, 0] = _compress(kc_ref, pek_ref, w1k_ref, w2k_ref).astype(BF16)
    vcmpt_ref[0, 0] = _compress(vc_ref, pev_ref, w1v_ref, w2v_ref).T.astype(BF16)


def _nsa_prep(z, rope_c, rope_a, rope_b, pe_k, w1_k, w2_k, pe_v, w1_v, w2_v):
    b, s, _ = z.shape
    first = D_MODEL // HEAD_DIM
    zspec = lambda k: pl.BlockSpec((1, s, HEAD_DIM), lambda i, g, k=k: (i, 0, first + k * N_KV + g))
    tab = pl.BlockSpec((1, s, HEAD_DIM), lambda i, g: (i, 0, 0))
    c2 = lambda shape: pl.BlockSpec(shape, lambda i, g: (0, 0))
    c3 = lambda shape: pl.BlockSpec(shape, lambda i, g: (0, 0, 0))
    out4 = lambda shape: pl.BlockSpec((1, 1) + shape, lambda i, g: (i, g, 0, 0))
    out5 = lambda shape: pl.BlockSpec((1, 1) + shape, lambda i, g: (i, g, 0, 0, 0))
    w1 = lambda w: w.reshape(CMP_LEN, HEAD_DIM, CMP_HIDDEN)
    return pl.pallas_call(
        _nsa_prep_kernel,
        grid=(b, N_KV),
        in_specs=[zspec(0), zspec(1), zspec(2), zspec(3), zspec(4), zspec(5), tab, tab, tab,
                  c2((CMP_LEN, HEAD_DIM)), c3((CMP_LEN, HEAD_DIM, CMP_HIDDEN)), c2((CMP_HIDDEN, HEAD_DIM)),
                  c2((CMP_LEN, HEAD_DIM)), c3((CMP_LEN, HEAD_DIM, CMP_HIDDEN)), c2((CMP_HIDDEN, HEAD_DIM))],
        out_specs=[out4((s, HEAD_DIM)), out5((s // SEL_TK, HEAD_DIM, SEL_TK)),
                   out4((s, HEAD_DIM)), out5((s // WIN_TK, HEAD_DIM, WIN_TK)),
                   out4((N_HALF, HEAD_DIM)), out4((HEAD_DIM, N_HALF))],
        out_shape=[jax.ShapeDtypeStruct((b, N_KV, s, HEAD_DIM), BF16),
                   jax.ShapeDtypeStruct((b, N_KV, s // SEL_TK, HEAD_DIM, SEL_TK), BF16),
                   jax.ShapeDtypeStruct((b, N_KV, s, HEAD_DIM), BF16),
                   jax.ShapeDtypeStruct((b, N_KV, s // WIN_TK, HEAD_DIM, WIN_TK), BF16),
                   jax.ShapeDtypeStruct((b, N_KV, N_HALF, HEAD_DIM), BF16),
                   jax.ShapeDtypeStruct((b, N_KV, HEAD_DIM, N_HALF), BF16)],
        compiler_params=_cparams(("parallel", "arbitrary"), 22),
        name="nsa_prep",
    )(z, z, z, z, z, z, rope_c, rope_a, rope_b,
      pe_k, w1(w1_k).astype(BF16), w2_k.astype(BF16), pe_v, w1(w1_v).astype(BF16), w2_v.astype(BF16))


Q_SCALE = SCALE * 1.4426950408889634


def _scores_t(k, q):
    return lax.dot_general(k, q, (((1,), (1,)), ((), ())), preferred_element_type=F32)


def _online_attend(q, k_ref, vt_ref, sel, tpos, tile, n_tiles):
    rows = q.shape[0]
    m = jnp.full((1, rows), NEG, F32)
    l = jnp.zeros((1, rows), F32)
    acc = jnp.zeros((HEAD_DIM, rows), F32)
    for kt in range(n_tiles):
        blocks = tile // SEL_BLOCK
        sel_rows = sel[kt * blocks:(kt + 1) * blocks, :]
        visible = jnp.broadcast_to(sel_rows[:, None, :], (blocks, SEL_BLOCK, sel.shape[1])
                                   ).reshape(tile, sel.shape[1]) > 0.5
        if kt == n_tiles - 1:
            kpos = kt * tile + lax.broadcasted_iota(jnp.int32, visible.shape, 0)
            visible = visible & (kpos <= tpos)
        bias = jnp.where(visible, 0.0, NEG)
        s = (_scores_t(k_ref[0, 0, kt * tile:(kt + 1) * tile, :], q)
             + jnp.concatenate([bias] * (rows // bias.shape[1]), axis=1))
        m_new = jnp.maximum(m, jnp.max(s, axis=0, keepdims=True))
        alpha = jnp.exp2(m - m_new)
        p = jnp.exp2(s - m_new)
        l = alpha * l + jnp.sum(p, axis=0, keepdims=True)
        acc = alpha * acc + jnp.dot(vt_ref[0, 0, kt], p.astype(BF16), preferred_element_type=F32)
        m = m_new
    return acc / l


def _nsa_attn_kernel(q_ref, zg_ref, rc_ref, ra_ref, rb_ref, ksr_ref, vst_ref, kwr_ref, vwt_ref,
                     kcmp_ref, vcmpt_ref, ovt_ref, o_ref, gate_scr, osel_scr, *, tq, seq):
    grp = pl.program_id(1)
    t0 = pl.program_id(2) * tq
    rows = HPG * tq
    heads = lambda x: jnp.concatenate([x] * HPG, axis=1)

    c, a, b = rc_ref[0], ra_ref[0], rb_ref[0]
    q_plain, q_rot = [], []
    for h in range(HPG):
        qh = q_ref[0, :, h * HEAD_DIM:(h + 1) * HEAD_DIM] * Q_SCALE
        q_plain.append(qh.astype(BF16))
        q_rot.append(_rope(qh, c, a, b).astype(BF16))
    q_plain = jnp.concatenate(q_plain, axis=0)
    q_rot = jnp.concatenate(q_rot, axis=0)

    tpos = t0 + lax.broadcasted_iota(jnp.int32, (1, tq), 1)

    s_c = _scores_t(kcmp_ref[0, 0], q_plain)
    n_idx = lax.broadcasted_iota(jnp.int32, (N_HALF, rows), 0)
    cmask = n_idx * CMP_STRIDE + (CMP_LEN - 1) <= heads(tpos)
    s_c = jnp.where(cmask, s_c, NEG)
    e = jnp.exp2(s_c - jnp.max(s_c, axis=0, keepdims=True))
    p_c = jnp.where(cmask, e / jnp.sum(e, axis=0, keepdims=True), 0.0)
    o_c = jnp.dot(vcmpt_ref[0, 0], p_c.astype(BF16), preferred_element_type=F32)

    span = WINDOW + tq
    k0 = pl.multiple_of(jnp.maximum(t0 - WINDOW, 0), WIN_TK)
    kp = k0 + lax.broadcasted_iota(jnp.int32, (span, tq), 0)
    w_bias = jnp.where((kp <= tpos) & (kp > tpos - WINDOW), 0.0, NEG)
    s_w = _scores_t(kwr_ref[0, 0, pl.ds(k0, span), :], q_rot) + heads(w_bias)
    p_w = jnp.exp2(s_w - jnp.max(s_w, axis=0, keepdims=True))
    l_w = jnp.sum(p_w, axis=0, keepdims=True)
    p_w = p_w.astype(BF16)
    o_w = jnp.zeros((HEAD_DIM, rows), F32)
    for i in range(span // WIN_TK):
        o_w = o_w + jnp.dot(vwt_ref[0, 0, k0 // WIN_TK + i], p_w[i * WIN_TK:(i + 1) * WIN_TK, :],
                            preferred_element_type=F32)
    o_w = o_w / l_w

    p_sum = p_c[:, 0:tq]
    for h in range(1, HPG):
        p_sum = p_sum + p_c[:, h * tq:(h + 1) * tq]
    p_hi = p_sum.astype(BF16)
    p_lo = (p_sum - p_hi.astype(F32)).astype(BF16)
    ovt = ovt_ref[...]
    imp = (jnp.dot(ovt, p_hi, preferred_element_type=F32)
           + jnp.dot(ovt, p_lo, preferred_element_type=F32))
    n_blk = seq // SEL_BLOCK
    j_idx = lax.broadcasted_iota(jnp.int32, (n_blk, tq), 0)
    cur = tpos // SEL_BLOCK
    forced = (j_idx == 0) | (j_idx == cur) | (j_idx == cur - 1)
    imp = imp + jnp.where(forced, FORCE_BONUS, 0.0)
    imp = jnp.where(j_idx * SEL_BLOCK <= tpos, imp, NEG)
    rank = jnp.zeros((n_blk, tq), jnp.int32)
    for i in range(n_blk):
        vi = imp[i:i + 1, :]
        ahead = (vi > imp) | ((vi == imp) & (j_idx > i))
        rank = rank + ahead.astype(jnp.int32)
    sel = (rank < min(SEL_TOPK, n_blk)).astype(F32)

    n_tiles =(t0 + tq + SEL_TK - 1) // SEL_TK
    for n in range(1, seq // SEL_TK + 1):
        @pl.when(n_tiles == n)
        def _(n=n):
            osel_scr[...] = _online_attend(q_rot, ksr_ref, vst_ref, sel, tpos, SEL_TK, n)
    o_s = osel_scr[...]

    gate_scr[...] = jax.nn.sigmoid(zg_ref[0]).T

    def gate(branch):
        return jnp.concatenate(
            [gate_scr[pl.ds(3 * (HPG * grp + h) + branch, 1), :] for h in range(HPG)], axis=1)

    o = gate(0) * o_c + gate(1) * o_s + gate(2) * o_w
    for h in range(HPG):
        o_ref[0, :, h * HEAD_DIM:(h + 1) * HEAD_DIM] = o[:, h * tq:(h + 1) * tq].T.astype(o_ref.dtype)


def _nsa_attn(z, rope_c, rope_a, rope_b, ksr, vst, kwr, vwt, kcmp, vcmpt, tq=256):
    b, s, _ = z.shape
    assert SEL_TK % tq == 0 and s % SEL_TK == 0
    n_blk = s // SEL_BLOCK
    n = np.arange(N_HALF)[None, :]
    j = np.arange(n_blk)[:, None]
    ovt = ((n < N_HALF - 1) & (n * CMP_STRIDE < (j + 1) * SEL_BLOCK)
           & (n * CMP_STRIDE + CMP_LEN - 1 >= j * SEL_BLOCK))
    gate_blk = NSA_MAIN // HEAD_DIM
    tab = pl.BlockSpec((1, tq, HEAD_DIM), lambda i, g, t: (i, t, 0))
    kv4 = lambda shape: pl.BlockSpec((1, 1) + shape, lambda i, g, t: (i, g, 0, 0))
    kv5 = lambda shape: pl.BlockSpec((1, 1) + shape, lambda i, g, t: (i, g, 0, 0, 0))
    return pl.pallas_call(
        functools.partial(_nsa_attn_kernel, tq=tq, seq=s),
        grid=(b, N_KV, s // tq),
        in_specs=[
            pl.BlockSpec((1, tq, HPG * HEAD_DIM), lambda i, g, t: (i, t, g)),
            pl.BlockSpec((1, tq, HEAD_DIM), lambda i, g, t: (i, t, gate_blk)),
            tab, tab, tab,
            kv4((s, HEAD_DIM)), kv5((s // SEL_TK, HEAD_DIM, SEL_TK)),
            kv4((s, HEAD_DIM)), kv5((s // WIN_TK, HEAD_DIM, WIN_TK)),
            kv4((N_HALF, HEAD_DIM)), kv4((HEAD_DIM, N_HALF)),
            pl.BlockSpec((n_blk, N_HALF), lambda i, g, t: (0, 0)),
        ],
        out_specs=pl.BlockSpec((1, tq, HPG * HEAD_DIM), lambda i, g, t: (i, t, g)),
        out_shape=jax.ShapeDtypeStruct((b, s, D_MODEL), BF16),
        scratch_shapes=[pltpu.VMEM((HEAD_DIM, tq), F32), pltpu.VMEM((HEAD_DIM, HPG * tq), F32)],
        compiler_params=_cparams(("parallel", "parallel", "arbitrary"), 24),
        name="nsa_attn",
    )(z, z, rope_c, rope_a, rope_b, ksr, vst, kwr, vwt, kcmp, vcmpt,
      jnp.asarray(ovt, BF16))


def _rope_tables(positions):
    half = ROT_DIM // 2
    inv = ROPE_THETA ** (-jnp.arange(0, ROT_DIM, 2, dtype=F32) / ROT_DIM)
    ang = positions.astype(F32)[..., None] * inv
    cos, sin = jnp.cos(ang), jnp.sin(ang)
    zeros = lambda w: jnp.zeros(cos.shape[:-1] + (w,), F32)
    c = jnp.concatenate([cos, cos, jnp.ones(cos.shape[:-1] + (HEAD_DIM - ROT_DIM,), F32)], axis=-1)
    a = jnp.concatenate([-sin, zeros(HEAD_DIM - half)], axis=-1)
    b = jnp.concatenate([zeros(half), sin, zeros(HEAD_DIM - ROT_DIM)], axis=-1)
    return c, a, b


def kernel(x, positions, l0_ffn1_norm, l0_ffn1_w_gate, l0_ffn1_w_up, l0_ffn1_w_down, l0_mix_norm, l0_w_in, l0_conv_w, l0_conv_b, l0_conv_ln_g, l0_conv_ln_b, l0_gmlp_ln_g, l0_gmlp_ln_b, l0_gmlp_ws, l0_gmlp_bs, l0_w_out, l0_ffn2_norm, l0_ffn2_w_gate, l0_ffn2_w_up, l0_ffn2_w_down, l1_ffn1_norm, l1_ffn1_w_gate, l1_ffn1_w_up, l1_ffn1_w_down, l1_mix_norm, l1_w_in, l1_cmp_pe_k, l1_cmp_w1_k, l1_cmp_w2_k, l1_cmp_pe_v, l1_cmp_w1_v, l1_cmp_w2_v, l1_w_out, l1_ffn2_norm, l1_ffn2_w_gate, l1_ffn2_w_up, l1_ffn2_w_down, final_norm):
    b, s, d = x.shape
    m = b * s
    bf = lambda w: w.astype(BF16)
    x2 = x.reshape(m, d)

    x2 = _ffn(x2, l0_ffn1_norm, l0_ffn1_w_gate, l0_ffn1_w_up, l0_ffn1_w_down)
    z = _norm_matmul(x2, l0_mix_norm, bf(l0_w_in), n_col_tiles=4, vmem_mib=44).reshape(b, s, -1)
    cat = _l0_mid(z, l0_conv_w, l0_conv_b, l0_conv_ln_g, l0_conv_ln_b,
                  l0_gmlp_ln_g, l0_gmlp_ln_b, l0_gmlp_ws, l0_gmlp_bs)
    x2 = _matmul_res(cat.reshape(m, -1), bf(l0_w_out), x2)
    x2 = _ffn(x2, l0_ffn2_norm, l0_ffn2_w_gate, l0_ffn2_w_up, l0_ffn2_w_down)

    x2 = _ffn(x2, l1_ffn1_norm, l1_ffn1_w_gate, l1_ffn1_w_up, l1_ffn1_w_down)
    w_in = jnp.pad(bf(l1_w_in), ((0, 0), (0, NSA_PAD - NSA_MAIN - NSA_GATES)))
    z = _norm_matmul(x2, l1_mix_norm, w_in, n_col_tiles=11, vmem_mib=30).reshape(b, s, -1)
    rope_c, rope_a, rope_b = _rope_tables(positions)
    ksr, vst, kwr, vwt, kcmp, vcmpt = _nsa_prep(
        z, rope_c, rope_a, rope_b, l1_cmp_pe_k, l1_cmp_w1_k, l1_cmp_w2_k,
        l1_cmp_pe_v, l1_cmp_w1_v, l1_cmp_w2_v)
    o = _nsa_attn(z, rope_c, rope_a, rope_b, ksr, vst, kwr, vwt, kcmp, vcmpt)
    x2 = _matmul_res(o.reshape(m, -1), bf(l1_w_out), x2)
    x2 = _ffn(x2, l1_ffn2_norm, l1_ffn2_w_gate, l1_ffn2_w_up, l1_ffn2_w_down,
              final_w=final_norm)
    return x2.reshape(b, s, d)
```

```python
import functools

import numpy as np
import jax
import jax.numpy as jnp
from jax import lax
from jax.experimental import pallas as pl
from jax.experimental.pallas import tpu as pltpu

F32 = jnp.float32
BF16 = jnp.bfloat16

D_MODEL = 2048
D_FF = 5632
EPS = 1e-6
CONV_CH = 1024
CONV_WIDTH = 31
GMLP_CH = 1024
GMLP_GROUPS = 8
GMLP_GDIM = 128
GMLP_CHUNK = 128
HEAD_DIM = 128
N_HEADS = 16
N_KV = 4
HPG = 4
KV_W = N_KV * HEAD_DIM
ROT_DIM = 32
ROPE_THETA = 500000.0
CMP_LEN = 32
CMP_STRIDE = 16
CMP_HIDDEN = 256
SEL_BLOCK = 64
SEL_TOPK = 16
WINDOW = 512
FORCE_BONUS = 1e3
NEG = -1e30
NSA_MAIN = D_MODEL + 6 * KV_W
NSA_GATES = 3 * N_HEADS
NSA_PAD = 5376
SCALE = HEAD_DIM ** -0.5

MIB = 1024 * 1024


def _cparams(semantics, vmem_mib):
    return pltpu.CompilerParams(dimension_semantics=semantics,
                                vmem_limit_bytes=vmem_mib * MIB)


def _rms_scale(x):
    return x * lax.rsqrt(jnp.mean(x * x, axis=-1, keepdims=True) + EPS)


DOWN_TN = 512


def _ffn_kernel(x_ref, nw_ref, wg_ref, wu_ref, wd_ref, *rest, final):
    if final:
        fn_ref, o_ref, h_scr = rest
    else:
        o_ref, h_scr = rest
    f = pl.program_id(1)

    @pl.when(f == 0)
    def _():
        h_scr[...] = (_rms_scale(x_ref[...]) * nw_ref[...]).astype(BF16)
        o_ref[...] = jnp.zeros(o_ref.shape, F32)

    h = h_scr[...]
    g = jnp.dot(h, wg_ref[...].astype(BF16), preferred_element_type=F32)
    u = jnp.dot(h, wu_ref[...].astype(BF16), preferred_element_type=F32)
    a = (g * jax.nn.sigmoid(g) * u).astype(BF16)
    for c in range(0, o_ref.shape[1], DOWN_TN):
        o_ref[:, c:c + DOWN_TN] += jnp.dot(a, wd_ref[:, c:c + DOWN_TN].astype(BF16),
                                           preferred_element_type=F32)

    @pl.when(f == pl.num_programs(1) - 1)
    def _():
        y = x_ref[...] + 0.5 * o_ref[...]
        if final:
            y = _rms_scale(y) * fn_ref[...]
        o_ref[...] = y


def _ffn(x2d, norm_w, wg, wu, wd, final_w=None, tm=1024, tf=256):
    m, d = x2d.shape
    dff = wg.shape[1]
    final = final_w is not None
    in_specs = [
        pl.BlockSpec((tm, d), lambda i, j: (i, 0)),
        pl.BlockSpec((1, d), lambda i, j: (0, 0)),
        pl.BlockSpec((d, tf), lambda i, j: (0, j)),
        pl.BlockSpec((d, tf), lambda i, j: (0, j)),
        pl.BlockSpec((tf, d), lambda i, j: (j, 0)),
    ]
    args = [x2d, norm_w.reshape(1, d), wg, wu, wd]
    if final:
        in_specs.append(pl.BlockSpec((1, d), lambda i, j: (0, 0)))
        args.append(final_w.reshape(1, d))
    return pl.pallas_call(
        functools.partial(_ffn_kernel, final=final),
        grid=(m // tm, dff // tf),
        in_specs=in_specs,
        out_specs=pl.BlockSpec((tm, d), lambda i, j: (i, 0)),
        out_shape=jax.ShapeDtypeStruct((m, d), F32),
        scratch_shapes=[pltpu.VMEM((tm, d), BF16)],
        compiler_params=_cparams(("parallel", "arbitrary"), 58),
        name="ffn",
    )(*args)


def _norm_matmul_kernel(x_ref, nw_ref, w_ref, o_ref, h_scr):
    @pl.when(pl.program_id(1) == 0)
    def _():
        h_scr[...] = (_rms_scale(x_ref[...]) * nw_ref[...]).astype(BF16)

    o_ref[...] = jnp.dot(h_scr[...], w_ref[...], preferred_element_type=F32)


def _norm_matmul(x2d, norm_w, w, n_col_tiles, vmem_mib, tm=1024):
    m, d = x2d.shape
    n = w.shape[1]
    tn = n // n_col_tiles
    assert tn * n_col_tiles == n and tn % HEAD_DIM == 0
    return pl.pallas_call(
        _norm_matmul_kernel,
        grid=(m // tm, n // tn),
        in_specs=[
            pl.BlockSpec((tm, d), lambda i, j: (i, 0)),
            pl.BlockSpec((1, d), lambda i, j: (0, 0)),
            pl.BlockSpec((d, tn), lambda i, j: (0, j)),
        ],
        out_specs=pl.BlockSpec((tm, tn), lambda i, j: (i, j)),
        out_shape=jax.ShapeDtypeStruct((m, n), F32),
        scratch_shapes=[pltpu.VMEM((tm, d), BF16)],
        compiler_params=_cparams(("parallel", "arbitrary"), vmem_mib),
        name="norm_matmul",
    )(x2d, norm_w.reshape(1, d), w)


def _matmul_res_kernel(a_ref, w_ref, r_ref, o_ref):
    o_ref[...] = r_ref[...] + jnp.dot(a_ref[...], w_ref[...], preferred_element_type=F32)


def _matmul_res(a, w, res, tm=512, tn=1024):
    m, k = a.shape
    n = w.shape[1]
    return pl.pallas_call(
        _matmul_res_kernel,
        grid=(m // tm, n // tn),
        in_specs=[
            pl.BlockSpec((tm, k), lambda i, j: (i, 0)),
            pl.BlockSpec((k, tn), lambda i, j: (0, j)),
            pl.BlockSpec((tm, tn), lambda i, j: (i, j)),
        ],
        out_specs=pl.BlockSpec((tm, tn), lambda i, j: (i, j)),
        out_shape=jax.ShapeDtypeStruct((m, n), F32),
        compiler_params=_cparams(("parallel", "arbitrary"), 24),
        name="matmul_res",
    )(a, w, res)


CONV_HALO = 32
SUBLANES = 8


def _layernorm(x, g, b):
    mu = jnp.mean(x, axis=-1, keepdims=True)
    xc = x - mu
    var = jnp.mean(xc * xc, axis=-1, keepdims=True)
    return xc * lax.rsqrt(var + EPS) * g + b


def _l0_mid_kernel(av_ref, ag_ref, bu_ref, bv_ref, cw_ref, cb_ref, clg_ref, clb_ref,
                   glg_ref, glb_ref, ws_ref, bst_ref, o_ref, abuf, shifted, *, ts):
    s = pl.program_id(1)

    @pl.when(s == 0)
    def _():
        abuf[0:CONV_HALO, :] = jnp.zeros((CONV_HALO, CONV_CH), F32)

    abuf[CONV_HALO:CONV_HALO + ts, :] = av_ref[0] * jax.nn.sigmoid(ag_ref[0])
    span = CONV_HALO + ts - SUBLANES
    for r in range(1, SUBLANES):
        shifted[r - 1] = abuf[r:r + span, :]
    base = CONV_HALO - (CONV_WIDTH - 1)
    acc = jnp.zeros((ts, CONV_CH), F32)
    for j in range(CONV_WIDTH):
        q, r = divmod(base + j, SUBLANES)
        src = abuf if r == 0 else shifted.at[r - 1]
        rows = src[SUBLANES * q:SUBLANES * q + ts, :]
        acc = acc + rows * cw_ref[j:j + 1, :]
    acc = acc + cb_ref[...]
    a = _layernorm(acc, clg_ref[...], clb_ref[...])
    o_ref[0, :, 0:CONV_CH] = (a * jax.nn.sigmoid(a)).astype(o_ref.dtype)
    abuf[0:CONV_HALO, :] = abuf[ts:ts + CONV_HALO, :]

    row = lax.broadcasted_iota(jnp.int32, (GMLP_CHUNK, GMLP_CHUNK), 0)
    col = lax.broadcasted_iota(jnp.int32, (GMLP_CHUNK, GMLP_CHUNK), 1)
    tri = col <= row
    for g in range(GMLP_GROUPS):
        lo = g * GMLP_GDIM
        w_g = jnp.where(tri, ws_ref[g], 0.0).astype(BF16)
        bias_g = bst_ref[:, g:g + 1]
        ln_g = glg_ref[:, lo:lo + GMLP_GDIM]
        ln_b = glb_ref[:, lo:lo + GMLP_GDIM]
        for c in range(ts // GMLP_CHUNK):
            r0 = c * GMLP_CHUNK
            u = jax.nn.gelu(bu_ref[0, r0:r0 + GMLP_CHUNK, lo:lo + GMLP_GDIM])
            v = jax.nn.gelu(bv_ref[0, r0:r0 + GMLP_CHUNK, lo:lo + GMLP_GDIM])
            v = _layernorm(v, ln_g, ln_b)
            sv = jnp.dot(w_g, v.astype(BF16), preferred_element_type=F32) + bias_g
            o_ref[0, r0:r0 + GMLP_CHUNK, CONV_CH + lo:CONV_CH + lo + GMLP_GDIM] = (
                (u * sv).astype(o_ref.dtype))


def _l0_mid(z, conv_w, conv_b, conv_ln_g, conv_ln_b, gmlp_ln_g, gmlp_ln_b, gmlp_ws, gmlp_bs, ts=256):
    b, s, _ = z.shape
    vec = lambda a: a.reshape(1, -1)
    const2 = lambda shape: pl.BlockSpec(shape, lambda i, j: (0, 0))
    zspec = lambda k: pl.BlockSpec((1, ts, CONV_CH), lambda i, j, k=k: (i, j, k))
    return pl.pallas_call(
        functools.partial(_l0_mid_kernel, ts=ts),
        grid=(b, s // ts),
        in_specs=[
            zspec(0), zspec(1), zspec(2), zspec(3),
            const2((CONV_WIDTH, CONV_CH)),
            const2((1, CONV_CH)), const2((1, CONV_CH)), const2((1, CONV_CH)),
            const2((1, GMLP_CH)), const2((1, GMLP_CH)),
            pl.BlockSpec((GMLP_GROUPS, GMLP_CHUNK, GMLP_CHUNK), lambda i, j: (0, 0, 0)),
            const2((GMLP_CHUNK, GMLP_GROUPS)),
        ],
        out_specs=pl.BlockSpec((1, ts, CONV_CH + GMLP_CH), lambda i, j: (i, j, 0)),
        out_shape=jax.ShapeDtypeStruct((b, s, CONV_CH + GMLP_CH), BF16),
        scratch_shapes=[pltpu.VMEM((CONV_HALO + ts, CONV_CH), F32),
                        pltpu.VMEM((SUBLANES - 1, CONV_HALO + ts - SUBLANES, CONV_CH), F32)],
        compiler_params=_cparams(("parallel", "arbitrary"), 16),
        name="l0_mid",
    )(z, z, z, z, conv_w.reshape(CONV_WIDTH, CONV_CH), vec(conv_b), vec(conv_ln_g), vec(conv_ln_b),
      vec(gmlp_ln_g), vec(gmlp_ln_b), gmlp_ws, gmlp_bs.T)


SEL_TK = 512
WIN_TK = 128
N_HALF = 128


def _rope(x, c, a, b):
    return x * c + pltpu.roll(x, HEAD_DIM - ROT_DIM // 2, 1) * a + pltpu.roll(x, ROT_DIM // 2, 1) * b


def _compress(k_ref, pe_ref, w1_ref, w2_ref):
    half = CMP_LEN // 2
    p = jnp.zeros((N_HALF, CMP_HIDDEN), F32)
    q = jnp.zeros((N_HALF, CMP_HIDDEN), F32)
    for l in range(half):
        rows = k_ref[0, pl.ds(l, N_HALF, stride=CMP_STRIDE), :]
        p = p + jnp.dot((rows + pe_ref[l:l + 1, :]).astype(BF16), w1_ref[l],
                        preferred_element_type=F32)
        q = q + jnp.dot((rows + pe_ref[half + l:half + l + 1, :]).astype(BF16), w1_ref[half + l],
                        preferred_element_type=F32)
    hid = p + pltpu.roll(q, N_HALF - 1, 0)
    return jnp.dot(jax.nn.gelu(hid).astype(BF16), w2_ref[...], preferred_element_type=F32)


def _nsa_prep_kernel(kc_ref, vc_ref, ks_ref, vs_ref, kw_ref, vw_ref, rc_ref, ra_ref, rb_ref,
                     pek_ref, w1k_ref, w2k_ref, pev_ref, w1v_ref, w2v_ref,
                     ksr_ref, vst_ref, kwr_ref, vwt_ref, kcmp_ref, vcmpt_ref):
    c, a, b = rc_ref[0], ra_ref[0], rb_ref[0]
    ksr_ref[0, 0] = _rope(ks_ref[0], c, a, b).astype(BF16)
    kwr_ref[0, 0] = _rope(kw_ref[0], c, a, b).astype(BF16)
    s = vs_ref.shape[1]
    for t in range(s // SEL_TK):
        vst_ref[0, 0, t] = vs_ref[0, t * SEL_TK:(t + 1) * SEL_TK, :].T.astype(BF16)
    for t in range(s // WIN_TK):
        vwt_ref[0, 0, t] = vw_ref[0, t * WIN_TK:(t + 1) * WIN_TK, :].T.astype(BF16)
    kcmp_ref[0, 0] = _compress(kc_ref, pek_ref, w1k_ref, w2k_ref).astype(BF16)
    vcmpt_ref[0, 0] = _compress(vc_ref, pev_ref, w1v_ref, w2v_ref).T.astype(BF16)


def _nsa_prep(z, rope_c, rope_a, rope_b, pe_k, w1_k, w2_k, pe_v, w1_v, w2_v):
    b, s, _ = z.shape
    first = D_MODEL // HEAD_DIM
    zspec = lambda k: pl.BlockSpec((1, s, HEAD_DIM), lambda i, g, k=k: (i, 0, first + k * N_KV + g))
    tab = pl.BlockSpec((1, s, HEAD_DIM), lambda i, g: (i, 0, 0))
    c2 = lambda shape: pl.BlockSpec(shape, lambda i, g: (0, 0))
    c3 = lambda shape: pl.BlockSpec(shape, lambda i, g: (0, 0, 0))
    out4 = lambda shape: pl.BlockSpec((1, 1) + shape, lambda i, g: (i, g, 0, 0))
    out5 = lambda shape: pl.BlockSpec((1, 1) + shape, lambda i, g: (i, g, 0, 0, 0))
    w1 = lambda w: w.reshape(CMP_LEN, HEAD_DIM, CMP_HIDDEN)
    return pl.pallas_call(
        _nsa_prep_kernel,
        grid=(b, N_KV),
        in_specs=[zspec(0), zspec(1), zspec(2), zspec(3), zspec(4), zspec(5), tab, tab, tab,
                  c2((CMP_LEN, HEAD_DIM)), c3((CMP_LEN, HEAD_DIM, CMP_HIDDEN)), c2((CMP_HIDDEN, HEAD_DIM)),
                  c2((CMP_LEN, HEAD_DIM)), c3((CMP_LEN, HEAD_DIM, CMP_HIDDEN)), c2((CMP_HIDDEN, HEAD_DIM))],
        out_specs=[out4((s, HEAD_DIM)), out5((s // SEL_TK, HEAD_DIM, SEL_TK)),
                   out4((s, HEAD_DIM)), out5((s // WIN_TK, HEAD_DIM, WIN_TK)),
                   out4((N_HALF, HEAD_DIM)), out4((HEAD_DIM, N_HALF))],
        out_shape=[jax.ShapeDtypeStruct((b, N_KV, s, HEAD_DIM), BF16),
                   jax.ShapeDtypeStruct((b, N_KV, s // SEL_TK, HEAD_DIM, SEL_TK), BF16),
                   jax.ShapeDtypeStruct((b, N_KV, s, HEAD_DIM), BF16),
                   jax.ShapeDtypeStruct((b, N_KV, s // WIN_TK, HEAD_DIM, WIN_TK), BF16),
                   jax.ShapeDtypeStruct((b, N_KV, N_HALF, HEAD_DIM), BF16),
                   jax.ShapeDtypeStruct((b, N_KV, HEAD_DIM, N_HALF), BF16)],
        compiler_params=_cparams(("parallel", "arbitrary"), 22),
        name="nsa_prep",
    )(z, z, z, z, z, z, rope_c, rope_a, rope_b,
      pe_k, w1(w1_k).astype(BF16), w2_k.astype(BF16), pe_v, w1(w1_v).astype(BF16), w2_v.astype(BF16))


Q_SCALE = SCALE * 1.4426950408889634


def _scores_t(k, q):
    return lax.dot_general(k, q, (((1,), (1,)), ((), ())), preferred_element_type=F32)


def _online_attend(q, k_ref, vt_ref, sel, tpos, tile, n_tiles):
    rows = q.shape[0]
    m = jnp.full((1, rows), NEG, F32)
    l = jnp.zeros((1, rows), F32)
    acc = jnp.zeros((HEAD_DIM, rows), F32)
    for kt in range(n_tiles):
        blocks = tile // SEL_BLOCK
        sel_rows = sel[kt * blocks:(kt + 1) * blocks, :]
        visible = jnp.broadcast_to(sel_rows[:, None, :], (blocks, SEL_BLOCK, sel.shape[1])
                                   ).reshape(tile, sel.shape[1]) > 0.5
        if kt == n_tiles - 1:
            kpos = kt * tile + lax.broadcasted_iota(jnp.int32, visible.shape, 0)
            visible = visible & (kpos <= tpos)
        bias = jnp.where(visible, 0.0, NEG)
        s = (_scores_t(k_ref[0, 0, kt * tile:(kt + 1) * tile, :], q)
             + jnp.concatenate([bias] * (rows // bias.shape[1]), axis=1))
        m_new = jnp.maximum(m, jnp.max(s, axis=0, keepdims=True))
        alpha = jnp.exp2(m - m_new)
        p = jnp.exp2(s - m_new)
        l = alpha * l + jnp.sum(p, axis=0, keepdims=True)
        acc = alpha * acc + jnp.dot(vt_ref[0, 0, kt], p.astype(BF16), preferred_element_type=F32)
        m = m_new
    return acc / l


def _nsa_attn_kernel(q_ref, zg_ref, rc_ref, ra_ref, rb_ref, ksr_ref, vst_ref, kwr_ref, vwt_ref,
                     kcmp_ref, vcmpt_ref, ovt_ref, o_ref, gate_scr, osel_scr, *, tq, seq):
    grp = pl.program_id(1)
    t0 = pl.program_id(2) * tq
    rows = HPG * tq
    heads = lambda x: jnp.concatenate([x] * HPG, axis=1)

    c, a, b = rc_ref[0], ra_ref[0], rb_ref[0]
    q_plain, q_rot = [], []
    for h in range(HPG):
        qh = q_ref[0, :, h * HEAD_DIM:(h + 1) * HEAD_DIM] * Q_SCALE
        q_plain.append(qh.astype(BF16))
        q_rot.append(_rope(qh, c, a, b).astype(BF16))
    q_plain = jnp.concatenate(q_plain, axis=0)
    q_rot = jnp.concatenate(q_rot, axis=0)

    tpos = t0 + lax.broadcasted_iota(jnp.int32, (1, tq), 1)

    s_c = _scores_t(kcmp_ref[0, 0], q_plain)
    n_idx = lax.broadcasted_iota(jnp.int32, (N_HALF, rows), 0)
    cmask = n_idx * CMP_STRIDE + (CMP_LEN - 1) <= heads(tpos)
    s_c = jnp.where(cmask, s_c, NEG)
    e = jnp.exp2(s_c - jnp.max(s_c, axis=0, keepdims=True))
    p_c = jnp.where(cmask, e / jnp.sum(e, axis=0, keepdims=True), 0.0)
    o_c = jnp.dot(vcmpt_ref[0, 0], p_c.astype(BF16), preferred_element_type=F32)

    span = WINDOW + tq
    k0 = pl.multiple_of(jnp.maximum(t0 - WINDOW, 0), WIN_TK)
    kp = k0 + lax.broadcasted_iota(jnp.int32, (span, tq), 0)
    w_bias = jnp.where((kp <= tpos) & (kp > tpos - WINDOW), 0.0, NEG)
    s_w = _scores_t(kwr_ref[0, 0, pl.ds(k0, span), :], q_rot) + heads(w_bias)
    p_w = jnp.exp2(s_w - jnp.max(s_w, axis=0, keepdims=True))
    l_w = jnp.sum(p_w, axis=0, keepdims=True)
    p_w = p_w.astype(BF16)
    o_w = jnp.zeros((HEAD_DIM, rows), F32)
    for i in range(span // WIN_TK):
        o_w = o_w + jnp.dot(vwt_ref[0, 0, k0 // WIN_TK + i], p_w[i * WIN_TK:(i + 1) * WIN_TK, :],
                            preferred_element_type=F32)
    o_w = o_w / l_w

    p_sum = p_c[:, 0:tq]
    for h in range(1, HPG):
        p_sum = p_sum + p_c[:, h * tq:(h + 1) * tq]
    p_hi = p_sum.astype(BF16)
    p_lo = (p_sum - p_hi.astype(F32)).astype(BF16)
    ovt = ovt_ref[...]
    imp = (jnp.dot(ovt, p_hi, preferred_element_type=F32)
           + jnp.dot(ovt, p_lo, preferred_element_type=F32))
    n_blk = seq // SEL_BLOCK
    j_idx = lax.broadcasted_iota(jnp.int32, (n_blk, tq), 0)
    cur = tpos // SEL_BLOCK
    forced = (j_idx == 0) | (j_idx == cur) | (j_idx == cur - 1)
    imp = imp + jnp.where(forced, FORCE_BONUS, 0.0)
    imp = jnp.where(j_idx * SEL_BLOCK <= tpos, imp, NEG)
    rank = jnp.zeros((n_blk, tq), jnp.int32)
    for i in range(n_blk):
        vi = imp[i:i + 1, :]
        ahead = (vi > imp) | ((vi == imp) & (j_idx > i))
        rank = rank + ahead.astype(jnp.int32)
    sel = (rank < min(SEL_TOPK, n_blk)).astype(F32)

    n_tiles =(t0 + tq + SEL_TK - 1) // SEL_TK
    for n in range(1, seq // SEL_TK + 1):
        @pl.when(n_tiles == n)
        def _(n=n):
            osel_scr[...] = _online_attend(q_rot, ksr_ref, vst_ref, sel, tpos, SEL_TK, n)
    o_s = osel_scr[...]

    gate_scr[...] = jax.nn.sigmoid(zg_ref[0]).T

    def gate(branch):
        return jnp.concatenate(
            [gate_scr[pl.ds(3 * (HPG * grp + h) + branch, 1), :] for h in range(HPG)], axis=1)

    o = gate(0) * o_c + gate(1) * o_s + gate(2) * o_w
    for h in range(HPG):
        o_ref[0, :, h * HEAD_DIM:(h + 1) * HEAD_DIM] = o[:, h * tq:(h + 1) * tq].T.astype(o_ref.dtype)


def _nsa_attn(z, rope_c, rope_a, rope_b, ksr, vst, kwr, vwt, kcmp, vcmpt, tq=256):
    b, s, _ = z.shape
    assert SEL_TK % tq == 0 and s % SEL_TK == 0
    n_blk = s // SEL_BLOCK
    n = np.arange(N_HALF)[None, :]
    j = np.arange(n_blk)[:, None]
    ovt = ((n < N_HALF - 1) & (n * CMP_STRIDE < (j + 1) * SEL_BLOCK)
           & (n * CMP_STRIDE + CMP_LEN - 1 >= j * SEL_BLOCK))
    gate_blk = NSA_MAIN // HEAD_DIM
    tab = pl.BlockSpec((1, tq, HEAD_DIM), lambda i, g, t: (i, t, 0))
    kv4 = lambda shape: pl.BlockSpec((1, 1) + shape, lambda i, g, t: (i, g, 0, 0))
    kv5 = lambda shape: pl.BlockSpec((1, 1) + shape, lambda i, g, t: (i, g, 0, 0, 0))
    return pl.pallas_call(
        functools.partial(_nsa_attn_kernel, tq=tq, seq=s),
        grid=(b, N_KV, s // tq),
        in_specs=[
            pl.BlockSpec((1, tq, HPG * HEAD_DIM), lambda i, g, t: (i, t, g)),
            pl.BlockSpec((1, tq, HEAD_DIM), lambda i, g, t: (i, t, gate_blk)),
            tab, tab, tab,
            kv4((s, HEAD_DIM)), kv5((s // SEL_TK, HEAD_DIM, SEL_TK)),
            kv4((s, HEAD_DIM)), kv5((s // WIN_TK, HEAD_DIM, WIN_TK)),
            kv4((N_HALF, HEAD_DIM)), kv4((HEAD_DIM, N_HALF)),
            pl.BlockSpec((n_blk, N_HALF), lambda i, g, t: (0, 0)),
        ],
        out_specs=pl.BlockSpec((1, tq, HPG * HEAD_DIM), lambda i, g, t: (i, t, g)),
        out_shape=jax.ShapeDtypeStruct((b, s, D_MODEL), BF16),
        scratch_shapes=[pltpu.VMEM((HEAD_DIM, tq), F32), pltpu.VMEM((HEAD_DIM, HPG * tq), F32)],
        compiler_params=_cparams(("parallel", "parallel", "arbitrary"), 24),
        name="nsa_attn",
    )(z, z, rope_c, rope_a, rope_b, ksr, vst, kwr, vwt, kcmp, vcmpt,
      jnp.asarray(ovt, BF16))


def _rope_tables(positions):
    half = ROT_DIM // 2
    inv = ROPE_THETA ** (-jnp.arange(0, ROT_DIM, 2, dtype=F32) / ROT_DIM)
    ang = positions.astype(F32)[..., None] * inv
    cos, sin = jnp.cos(ang), jnp.sin(ang)
    zeros = lambda w: jnp.zeros(cos.shape[:-1] + (w,), F32)
    c = jnp.concatenate([cos, cos, jnp.ones(cos.shape[:-1] + (HEAD_DIM - ROT_DIM,), F32)], axis=-1)
    a = jnp.concatenate([-sin, zeros(HEAD_DIM - half)], axis=-1)
    b = jnp.concatenate([zeros(half), sin, zeros(HEAD_DIM - ROT_DIM)], axis=-1)
    return c, a, b


def kernel(x, positions, l0_ffn1_norm, l0_ffn1_w_gate, l0_ffn1_w_up, l0_ffn1_w_down, l0_mix_norm, l0_w_in, l0_conv_w, l0_conv_b, l0_conv_ln_g, l0_conv_ln_b, l0_gmlp_ln_g, l0_gmlp_ln_b, l0_gmlp_ws, l0_gmlp_bs, l0_w_out, l0_ffn2_norm, l0_ffn2_w_gate, l0_ffn2_w_up, l0_ffn2_w_down, l1_ffn1_norm, l1_ffn1_w_gate, l1_ffn1_w_up, l1_ffn1_w_down, l1_mix_norm, l1_w_in, l1_cmp_pe_k, l1_cmp_w1_k, l1_cmp_w2_k, l1_cmp_pe_v, l1_cmp_w1_v, l1_cmp_w2_v, l1_w_out, l1_ffn2_norm, l1_ffn2_w_gate, l1_ffn2_w_up, l1_ffn2_w_down, final_norm):
    b, s, d = x.shape
    m = b * s
    bf = lambda w: w.astype(BF16)
    x2 = x.reshape(m, d)

    x2 = _ffn(x2, l0_ffn1_norm, l0_ffn1_w_gate, l0_ffn1_w_up, l0_ffn1_w_down)
    z = _norm_matmul(x2, l0_mix_norm, bf(l0_w_in), n_col_tiles=4, vmem_mib=44).reshape(b, s, -1)
    cat = _l0_mid(z, l0_conv_w, l0_conv_b, l0_conv_ln_g, l0_conv_ln_b,
                  l0_gmlp_ln_g, l0_gmlp_ln_b, l0_gmlp_ws, l0_gmlp_bs)
    x2 = _matmul_res(cat.reshape(m, -1), bf(l0_w_out), x2)
    x2 = _ffn(x2, l0_ffn2_norm, l0_ffn2_w_gate, l0_ffn2_w_up, l0_ffn2_w_down)

    x2 = _ffn(x2, l1_ffn1_norm, l1_ffn1_w_gate, l1_ffn1_w_up, l1_ffn1_w_down)
    w_in = jnp.pad(bf(l1_w_in), ((0, 0), (0, NSA_PAD - NSA_MAIN - NSA_GATES)))
    z = _norm_matmul(x2, l1_mix_norm, w_in, n_col_tiles=7, vmem_mib=36).reshape(b, s, -1)
    rope_c, rope_a, rope_b = _rope_tables(positions)
    ksr, vst, kwr, vwt, kcmp, vcmpt = _nsa_prep(
        z, rope_c, rope_a, rope_b, l1_cmp_pe_k, l1_cmp_w1_k, l1_cmp_w2_k,
        l1_cmp_pe_v, l1_cmp_w1_v, l1_cmp_w2_v)
    o = _nsa_attn(z, rope_c, rope_a, rope_b, ksr, vst, kwr, vwt, kcmp, vcmpt)
    x2 = _matmul_res(o.reshape(m, -1), bf(l1_w_out), x2)
    x2 = _ffn(x2, l1_ffn2_norm, l1_ffn2_w_gate, l1_ffn2_w_up, l1_ffn2_w_down,
              final_w=final_norm)
    return x2.reshape(b, s, d)
```
